```python
import jax, jax.numpy as jnp
from jax import lax
import numpy as np

D_MODEL = 1024
BATCH = 8
SEQ = 2048
DEPTH = 1

CHUNK = 64
HEAD_DIM = 64
H_SB = 8
H_CH = 8
W_SB = H_SB * HEAD_DIM
W_CH = H_CH * HEAD_DIM
MIX_WIDTH = W_SB + W_CH
LOOKBACK = 8
BAND = (LOOKBACK + 1) * CHUNK
REL_CLIP = 128
Q_BLOCK = 128
D_FF = 2816
PLE_DIM = 256
EPS = 1e-6
NEG_INF = -1e30

kernel_name = "hybrid_stickbreak_chunkattn_macaron_block"


def rms_norm(x, g):
    xf = x.astype(jnp.float32)
    y = xf * lax.rsqrt(jnp.mean(xf * xf, axis=-1, keepdims=True) + EPS)
    return (y * g.astype(jnp.float32)).astype(x.dtype)


def swiglu(x, w_gate, w_up, w_down):
    return (jax.nn.silu(x @ w_gate) * (x @ w_up)) @ w_down


def split_heads(t, n_heads):
    b, s, _ = t.shape
    return t.reshape(b, s, n_heads, HEAD_DIM).transpose(0, 2, 1, 3)


def merge_heads(t):
    b, h, s, d = t.shape
    return t.transpose(0, 2, 1, 3).reshape(b, s, h * d)


def stick_breaking_attention(q, k, v):
    b, h, s, d = q.shape
    nq = s // Q_BLOCK
    scale = d ** -0.5
    q_blocks = q.reshape(b, h, nq, Q_BLOCK, d).transpose(2, 0, 1, 3, 4)
    starts = jnp.arange(nq, dtype=jnp.int32) * Q_BLOCK
    key_pos = jnp.arange(s, dtype=jnp.int32)

    def one_block(args):
        q_blk, start = args
        z = jnp.einsum('bhqd,bhkd->bhqk', q_blk, k,
                       preferred_element_type=jnp.float32) * scale
        q_pos = start + jnp.arange(Q_BLOCK, dtype=jnp.int32)
        before = key_pos[None, :] < q_pos[:, None]
        log_fail = jnp.where(before, jax.nn.log_sigmoid(-z), 0.0)
        later = lax.cumsum(log_fail, axis=3, reverse=True) - log_fail
        log_a = jax.nn.log_sigmoid(z) + later
        a = jnp.where(before, jnp.exp(jnp.where(before, log_a, 0.0)), 0.0)
        return jnp.einsum('bhqk,bhkd->bhqd', a.astype(v.dtype), v)

    out = lax.map(one_block, (q_blocks, starts))
    return out.transpose(1, 2, 0, 3, 4).reshape(b, h, s, d)


def rel_bias_index():
    i = np.arange(CHUNK)[:, None]
    j = np.arange(BAND)[None, :]
    dist = i + LOOKBACK * CHUNK - j
    return jnp.asarray(np.clip(dist, -REL_CLIP, REL_CLIP) + REL_CLIP, dtype=jnp.int32)


def chunk_band_attention(q, k, v, rel_bias):
    b, h, s, d = q.shape
    nc = s // CHUNK
    scale = d ** -0.5
    qc = q.reshape(b, h, nc, CHUNK, d)

    def band(t):
        tc = t.reshape(b, h, nc, CHUNK, d)
        tp = jnp.pad(tc, ((0, 0), (0, 0), (LOOKBACK, 0), (0, 0), (0, 0)))
        return jnp.concatenate([tp[:, :, w:w + nc] for w in range(LOOKBACK + 1)], axis=3)

    kb, vb = band(k), band(v)
    bias = rel_bias.astype(jnp.float32)[:, rel_bias_index()]
    z = jnp.einsum('bhnqd,bhnkd->bhnqk', qc, kb,
                   preferred_element_type=jnp.float32) * scale + bias[None, :, None]
    slot_chunk = jnp.arange(BAND, dtype=jnp.int32) // CHUNK
    chunk_id = jnp.arange(nc, dtype=jnp.int32)
    valid = (chunk_id[:, None] + slot_chunk[None, :] - LOOKBACK) >= 0
    z = jnp.where(valid[None, None, :, None, :], z, NEG_INF)
    prob = jax.nn.softmax(z, axis=-1)
    o = jnp.einsum('bhnqk,bhnkd->bhnqd', prob.astype(vb.dtype), vb)
    return o.reshape(b, h, s, d)


def setup_inputs(seed: int = 0) -> dict:
    key = jax.random.key(seed)
    ks = jax.random.split(key, 24)
    f32 = jnp.float32

    def w(k, shape, fan_in):
        return jax.random.normal(k, shape, f32) * (fan_in ** -0.5)

    def gain(k, n):
        return 1.0 + 0.05 * jax.random.normal(k, (DEPTH, n), f32)

    return {
        "x": jax.random.normal(ks[0], (BATCH, SEQ, D_MODEL), f32),
        "p": jax.random.normal(ks[1], (DEPTH, BATCH, SEQ, PLE_DIM), f32),
        "g_ffn1_pre": gain(ks[2], D_MODEL),
        "g_ffn1_post": gain(ks[3], D_MODEL),
        "w_ffn1_gate": w(ks[4], (DEPTH, D_MODEL, D_FF), D_MODEL),
        "w_ffn1_up": w(ks[5], (DEPTH, D_MODEL, D_FF), D_MODEL),
        "w_ffn1_down": w(ks[6], (DEPTH, D_FF, D_MODEL), D_FF),
        "g_mix_pre": gain(ks[7], D_MODEL),
        "g_mix_post": gain(ks[8], D_MODEL),
        "w_in": w(ks[9], (DEPTH, D_MODEL, 3 * MIX_WIDTH), D_MODEL),
        "g_out_sb": gain(ks[10], W_SB),
        "g_out_ch": gain(ks[11], W_CH),
        "rel_bias": 0.02 * jax.random.normal(ks[12], (DEPTH, H_CH, 2 * REL_CLIP + 1), f32),
        "w_out": w(ks[13], (DEPTH, MIX_WIDTH, D_MODEL), MIX_WIDTH),
        "g_ffn2_pre": gain(ks[14], D_MODEL),
        "g_ffn2_post": gain(ks[15], D_MODEL),
        "w_ffn2_gate": w(ks[16], (DEPTH, D_MODEL, D_FF), D_MODEL),
        "w_ffn2_up": w(ks[17], (DEPTH, D_MODEL, D_FF), D_MODEL),
        "w_ffn2_down": w(ks[18], (DEPTH, D_FF, D_MODEL), D_FF),
        "w_ple_proj": w(ks[19], (DEPTH, PLE_DIM, D_MODEL), PLE_DIM),
        "w_ple_gate": w(ks[20], (DEPTH, D_MODEL, D_MODEL), D_MODEL),
        "g_ple_post": gain(ks[21], D_MODEL),
    }


def reference(x, p, g_ffn1_pre, g_ffn1_post, w_ffn1_gate, w_ffn1_up, w_ffn1_down,
              g_mix_pre, g_mix_post, w_in, g_out_sb, g_out_ch, rel_bias, w_out,
              g_ffn2_pre, g_ffn2_post, w_ffn2_gate, w_ffn2_up, w_ffn2_down,
              w_ple_proj, w_ple_gate, g_ple_post):
    h = x
    for i in range(DEPTH):
        f = swiglu(rms_norm(h, g_ffn1_pre[i]), w_ffn1_gate[i], w_ffn1_up[i], w_ffn1_down[i])
        h = h + 0.5 * rms_norm(f, g_ffn1_post[i])

        u = rms_norm(h, g_mix_pre[i])
        qkv = u @ w_in[i]
        q_a, k_a, v_a, q_b, k_b, v_b = jnp.split(
            qkv, np.cumsum([W_SB, W_SB, W_SB, W_CH, W_CH])[:5].tolist(), axis=-1)
        o_a = stick_breaking_attention(split_heads(q_a, H_SB), split_heads(k_a, H_SB),
                                       split_heads(v_a, H_SB))
        o_b = chunk_band_attention(split_heads(q_b, H_CH), split_heads(k_b, H_CH),
                                   split_heads(v_b, H_CH), rel_bias[i])
        mixed = jnp.concatenate([rms_norm(merge_heads(o_a), g_out_sb[i]),
                                 rms_norm(merge_heads(o_b), g_out_ch[i])], axis=-1)
        h = h + rms_norm(mixed @ w_out[i], g_mix_post[i])

        f = swiglu(rms_norm(h, g_ffn2_pre[i]), w_ffn2_gate[i], w_ffn2_up[i], w_ffn2_down[i])
        h = h + 0.5 * rms_norm(f, g_ffn2_post[i])

        e = (p[i] @ w_ple_proj[i]) * jax.nn.sigmoid(h @ w_ple_gate[i])
        h = h + rms_norm(e, g_ple_post[i])
    return h
```

```python
import functools

import jax
import jax.numpy as jnp
import numpy as np
from jax import lax
from jax.experimental import pallas as pl
from jax.experimental.pallas import tpu as pltpu

HEAD_DIM = 64
H_SB = 8
H_CH = 8
W_SB = H_SB * HEAD_DIM
W_CH = H_CH * HEAD_DIM
CHUNK = 64
LOOKBACK = 8
REL_CLIP = 128
EPS = 1e-6
NEG_INF = -1e30
SCALE = HEAD_DIM ** -0.5

LANES = 128
ROW_TILE = 512
FF_SPLIT = 2
SB_TILE = 256
CH_TQ = 128
CH_WIN = LOOKBACK * CHUNK + CH_TQ
CH_NBLK = CH_WIN // LANES
CH_TAB_BLKS = CH_NBLK + LOOKBACK * CHUNK // LANES
DENSE_VMEM_BYTES = 58 * 1024 * 1024
ATTN_VMEM_BYTES = 40 * 1024 * 1024

F32 = jnp.float32
BF16 = jnp.bfloat16


def _rms(x, g):
    ms = jnp.mean(x * x, axis=-1, keepdims=True)
    return x * lax.rsqrt(ms + EPS) * g


def _swiglu(u, wg_ref, wu_ref, wd_ref):
    d_ff = wg_ref.shape[1]
    slab = d_ff // FF_SPLIT
    f = None
    for c in range(FF_SPLIT):
        lo, hi = c * slab, (c + 1) * slab
        gate = jnp.dot(u, wg_ref[:, lo:hi], preferred_element_type=F32)
        up = jnp.dot(u, wu_ref[:, lo:hi], preferred_element_type=F32)
        act = (gate * jax.nn.sigmoid(gate) * up).astype(BF16)
        part = jnp.dot(act, wd_ref[lo:hi, :], preferred_element_type=F32)
        f = part if f is None else f + part
    return f


def _ffn_qkv_kernel(x_ref, gpre_ref, gpost_ref, wg_ref, wu_ref, wd_ref,
                    gmix_ref, win_ref, h_ref, qkv_ref):
    x = x_ref[...]
    u = _rms(x, gpre_ref[...]).astype(BF16)
    f = _swiglu(u, wg_ref, wu_ref, wd_ref)
    h = x + 0.5 * _rms(f, gpost_ref[...])
    h_ref[...] = h
    u2 = _rms(h, gmix_ref[...]).astype(BF16)
    qkv_ref[...] = jnp.dot(u2, win_ref[...], preferred_element_type=F32).astype(BF16)


def _resident(shape):
    return pl.BlockSpec(shape, lambda i: (0,) * len(shape), pipeline_mode=pl.Buffered(1))


def _ffn_qkv(x2, gpre, gpost, wg, wu, wd, gmix, win):
    n, d = x2.shape
    d_ff = wg.shape[1]
    d_qkv = win.shape[1]
    row = lambda w: pl.BlockSpec((ROW_TILE, w), lambda i: (i, 0))
    return pl.pallas_call(
        _ffn_qkv_kernel,
        grid=(n // ROW_TILE,),
        in_specs=[row(d), _resident((1, d)), _resident((1, d)),
                  _resident((d, d_ff)), _resident((d, d_ff)), _resident((d_ff, d)),
                  _resident((1, d)), _resident((d, d_qkv))],
        out_specs=[row(d), row(d_qkv)],
        out_shape=[jax.ShapeDtypeStruct((n, d), F32),
                   jax.ShapeDtypeStruct((n, d_qkv), BF16)],
        compiler_params=pltpu.CompilerParams(
            dimension_semantics=("arbitrary",), vmem_limit_bytes=DENSE_VMEM_BYTES),
        name="ffn1_qkv",
    )(x2, gpre, gpost, wg, wu, wd, gmix, win)


def _split_heads(q2):
    lane = lax.broadcasted_iota(jnp.int32, q2.shape, 1)
    zero = jnp.zeros_like(q2)
    qs = q2 * jnp.asarray(SCALE, q2.dtype)
    return jnp.where(lane < HEAD_DIM, qs, zero), jnp.where(lane >= HEAD_DIM, qs, zero)


def _sb_kernel(q_ref, k_ref, v_ref, o_ref):
    t = SB_TILE
    qi = pl.program_id(2)
    q_heads = _split_heads(q_ref[0])

    row = lax.broadcasted_iota(jnp.int32, (t, t), 0)
    col = lax.broadcasted_iota(jnp.int32, (t, t), 1)
    neg_upper = jnp.where(row >= col, -1.0, 0.0).astype(BF16)
    before = col < row

    def block(qh, kblk, vblk, diag):
        z = lax.dot_general(qh, kblk, (((1,), (1,)), ((), ())), preferred_element_type=F32)
        sp = jnp.maximum(z, 0.0) + jnp.log(1.0 + jnp.exp(-jnp.abs(z)))
        if diag:
            sp = jnp.where(before, sp, 0.0)
        sp_hi = sp.astype(BF16)
        sp_lo = (sp - sp_hi.astype(F32)).astype(BF16)
        c = (jnp.dot(sp_hi, neg_upper, preferred_element_type=F32)
             + jnp.dot(sp_lo, neg_upper, preferred_element_type=F32))
        p = jnp.exp(z + c)
        if diag:
            p = jnp.where(before, p, 0.0)
        pv = jnp.dot(p.astype(BF16), vblk, preferred_element_type=F32)
        return pv, c[:, 0:1]

    def load(kb):
        start = pl.multiple_of(kb * t, t)
        return k_ref[0, pl.ds(start, t), :], v_ref[0, pl.ds(start, t), :]

    kblk, vblk = load(qi)
    state = []
    for qh in q_heads:
        pv, tot = block(qh, kblk, vblk, True)
        state += [pv, tot]

    def body(i, st):
        kblk, vblk = load(qi - 1 - i)
        out = []
        for h, qh in enumerate(q_heads):
            acc, carry = st[2 * h], st[2 * h + 1]
            pv, tot = block(qh, kblk, vblk, False)
            out += [acc + jnp.exp(carry) * pv, carry + tot]
        return tuple(out)

    acc_a, _, acc_b, _ = lax.fori_loop(0, qi, body, tuple(state))
    lane = lax.broadcasted_iota(jnp.int32, acc_a.shape, 1)
    o_ref[0] = jnp.where(lane < HEAD_DIM, acc_a, acc_b)


def _sb_attention(qkv, b, s):
    t = SB_TILE
    npair = W_SB // LANES
    return pl.pallas_call(
        _sb_kernel,
        grid=(b, npair, s // t),
        in_specs=[pl.BlockSpec((1, t, LANES), lambda bi, hp, qi: (bi, qi, hp)),
                  pl.BlockSpec((1, s, LANES), lambda bi, hp, qi: (bi, 0, npair + hp)),
                  pl.BlockSpec((1, s, LANES), lambda bi, hp, qi: (bi, 0, 2 * npair + hp))],
        out_specs=pl.BlockSpec((1, t, LANES), lambda bi, hp, qi: (bi, qi, hp)),
        out_shape=jax.ShapeDtypeStruct((b, s, W_SB), F32),
        compiler_params=pltpu.CompilerParams(
            dimension_semantics=("arbitrary", "arbitrary", "arbitrary"),
            vmem_limit_bytes=ATTN_VMEM_BYTES),
        name="stickbreak_attn",
    )(qkv, qkv, qkv)


def _band_bias_table(rel_bias):
    width = CH_TAB_BLKS * LANES
    i = np.arange(CH_TQ)[:, None]
    j = np.arange(width)[None, :]
    dist = i + LOOKBACK * CHUNK - j
    idx = np.clip(dist, -REL_CLIP, REL_CLIP) + REL_CLIP
    dchunk = j // CHUNK - i // CHUNK
    valid = (dchunk >= 0) & (dchunk <= LOOKBACK) & (j < CH_WIN)
    tab = jnp.where(jnp.asarray(valid)[None], rel_bias.astype(F32)[:, idx], NEG_INF)
    h = rel_bias.shape[0]
    return tab.reshape(h, CH_TQ, CH_TAB_BLKS, LANES).transpose(0, 2, 1, 3)


def _ch_kernel(q_ref, k_ref, v_ref, tab_ref, o_ref):
    qi = pl.program_id(2)
    q_heads = _split_heads(q_ref[0])
    nominal = qi * CH_TQ - LOOKBACK * CHUNK
    start = pl.multiple_of(jnp.maximum(nominal, 0), LANES)
    shift_blk = (start - nominal) // LANES
    kwin = k_ref[0, pl.ds(start, CH_WIN), :]
    vwin = v_ref[0, pl.ds(start, CH_WIN), :]
    outs = []
    for h, qh in enumerate(q_heads):
        bias = jnp.concatenate(
            [tab_ref[h, shift_blk + o] for o in range(CH_NBLK)], axis=-1)
        z = lax.dot_general(qh, kwin, (((1,), (1,)), ((), ())),
                            preferred_element_type=F32) + bias
        m = jnp.max(z, axis=-1, keepdims=True)
        p = jnp.exp(z - m)
        l = jnp.sum(p, axis=-1, keepdims=True)
        pv = jnp.dot(p.astype(BF16), vwin, preferred_element_type=F32)
        outs.append(pv / l)
    lane = lax.broadcasted_iota(jnp.int32, outs[0].shape, 1)
    o_ref[0] = jnp.where(lane < HEAD_DIM, outs[0], outs[1])


def _ch_attention(qkv, tab, b, s):
    npair = W_CH // LANES
    base = 3 * W_SB // LANES
    return pl.pallas_call(
        _ch_kernel,
        grid=(b, npair, s // CH_TQ),
        in_specs=[pl.BlockSpec((1, CH_TQ, LANES), lambda bi, hp, qi: (bi, qi, base + hp)),
                  pl.BlockSpec((1, s, LANES), lambda bi, hp, qi: (bi, 0, base + npair + hp)),
                  pl.BlockSpec((1, s, LANES), lambda bi, hp, qi: (bi, 0, base + 2 * npair + hp)),
                  pl.BlockSpec((2, CH_TAB_BLKS, CH_TQ, LANES), lambda bi, hp, qi: (hp, 0, 0, 0))],
        out_specs=pl.BlockSpec((1, CH_TQ, LANES), lambda bi, hp, qi: (bi, qi, hp)),
        out_shape=jax.ShapeDtypeStruct((b, s, W_CH), F32),
        compiler_params=pltpu.CompilerParams(
            dimension_semantics=("arbitrary", "arbitrary", "arbitrary"),
            vmem_limit_bytes=ATTN_VMEM_BYTES),
        name="band_attn",
    )(qkv, qkv, qkv, tab)


def _tail_kernel(h_ref, oa_ref, ob_ref, p_ref, gsb_ref, gch_ref, wout_ref, gmixpost_ref,
                 gpre_ref, gpost_ref, wg_ref, wu_ref, wd_ref, wple_ref, wgate_ref, gple_ref,
                 out_ref):
    mixed = jnp.concatenate([_rms(oa_ref[...], gsb_ref[...]).astype(BF16),
                             _rms(ob_ref[...], gch_ref[...]).astype(BF16)], axis=-1)
    y = jnp.dot(mixed, wout_ref[...], preferred_element_type=F32)
    h = h_ref[...] + _rms(y, gmixpost_ref[...])
    u = _rms(h, gpre_ref[...]).astype(BF16)
    f = _swiglu(u, wg_ref, wu_ref, wd_ref)
    h = h + 0.5 * _rms(f, gpost_ref[...])
    proj = jnp.dot(p_ref[...].astype(BF16), wple_ref[...], preferred_element_type=F32)
    gate = jnp.dot(h.astype(BF16), wgate_ref[...], preferred_element_type=F32)
    e = proj * jax.nn.sigmoid(gate)
    out_ref[...] = h + _rms(e, gple_ref[...])


def _tail(h, oa, ob, p2, gsb, gch, wout, gmixpost, gpre, gpost, wg, wu, wd, wple, wgate, gple):
    n, d = h.shape
    d_ff = wg.shape[1]
    row = lambda w: pl.BlockSpec((ROW_TILE, w), lambda i: (i, 0))
    return pl.pallas_call(
        _tail_kernel,
        grid=(n // ROW_TILE,),
        in_specs=[row(d), row(oa.shape[1]), row(ob.shape[1]), row(p2.shape[1]),
                  _resident((1, oa.shape[1])), _resident((1, ob.shape[1])),
                  _resident(wout.shape), _resident((1, d)),
                  _resident((1, d)), _resident((1, d)),
                  _resident((d, d_ff)), _resident((d, d_ff)), _resident((d_ff, d)),
                  _resident(wple.shape), _resident(wgate.shape), _resident((1, d))],
        out_specs=row(d),
        out_shape=jax.ShapeDtypeStruct((n, d), F32),
        compiler_params=pltpu.CompilerParams(
            dimension_semantics=("arbitrary",), vmem_limit_bytes=DENSE_VMEM_BYTES),
        name="outproj_ffn2_ple",
    )(h, oa, ob, p2, gsb, gch, wout, gmixpost, gpre, gpost, wg, wu, wd, wple, wgate, gple)


def kernel(x, p, g_ffn1_pre, g_ffn1_post, w_ffn1_gate, w_ffn1_up, w_ffn1_down, g_mix_pre, g_mix_post, w_in, g_out_sb, g_out_ch, rel_bias, w_out, g_ffn2_pre, g_ffn2_post, w_ffn2_gate, w_ffn2_up, w_ffn2_down, w_ple_proj, w_ple_gate, g_ple_post):
    b, s, d = x.shape
    depth = p.shape[0]
    n = b * s
    gain = lambda g: g.astype(F32).reshape(1, -1)
    wcast = lambda w: w.astype(BF16)
    h = x.reshape(n, d)
    for i in range(depth):
        h1, qkv = _ffn_qkv(h, gain(g_ffn1_pre[i]), gain(g_ffn1_post[i]),
                           wcast(w_ffn1_gate[i]), wcast(w_ffn1_up[i]), wcast(w_ffn1_down[i]),
                           gain(g_mix_pre[i]), wcast(w_in[i]))
        qkv3 = qkv.reshape(b, s, qkv.shape[1])
        o_a = _sb_attention(qkv3, b, s).reshape(n, W_SB)
        o_b = _ch_attention(qkv3, _band_bias_table(rel_bias[i]), b, s).reshape(n, W_CH)
        h = _tail(h1, o_a, o_b, p[i].reshape(n, -1),
                  gain(g_out_sb[i]), gain(g_out_ch[i]), wcast(w_out[i]), gain(g_mix_post[i]),
                  gain(g_ffn2_pre[i]), gain(g_ffn2_post[i]),
                  wcast(w_ffn2_gate[i]), wcast(w_ffn2_up[i]), wcast(w_ffn2_down[i]),
                  wcast(w_ple_proj[i]), wcast(w_ple_gate[i]), gain(g_ple_post[i]))
    return h.reshape(b, s, d)
```

```python
import jax
import jax.numpy as jnp
import numpy as np
from jax import lax
from jax.experimental import pallas as pl
from jax.experimental.pallas import tpu as pltpu

HEAD_DIM = 64
H_SB = 8
H_CH = 8
W_SB = H_SB * HEAD_DIM
W_CH = H_CH * HEAD_DIM
CHUNK = 64
LOOKBACK = 8
REL_CLIP = 128
EPS = 1e-6
NEG_INF = -1e30
SCALE = HEAD_DIM ** -0.5

LANES = 128
ROW_TILE = 512
FF_SPLIT = 2
SB_TILE = 256
CH_TQ = 128
CH_WIN = LOOKBACK * CHUNK + CH_TQ
CH_NBLK = CH_WIN // LANES
CH_TAB_BLKS = CH_NBLK + LOOKBACK * CHUNK // LANES
EXP_ZERO_BELOW = -104.0
DENSE_VMEM_BYTES = 58 * 1024 * 1024
ATTN_VMEM_BYTES = 48 * 1024 * 1024

F32 = jnp.float32
BF16 = jnp.bfloat16


def _rms(x, g):
    ms = jnp.mean(x * x, axis=-1, keepdims=True)
    return x * lax.rsqrt(ms + EPS) * g


def _swiglu(u, wg_ref, wu_ref, wd_ref):
    d_ff = wg_ref.shape[1]
    slab = d_ff // FF_SPLIT
    f = None
    for c in range(FF_SPLIT):
        lo, hi = c * slab, (c + 1) * slab
        gate = jnp.dot(u, wg_ref[:, lo:hi], preferred_element_type=F32)
        up = jnp.dot(u, wu_ref[:, lo:hi], preferred_element_type=F32)
        act = (gate * jax.nn.sigmoid(gate) * up).astype(BF16)
        part = jnp.dot(act, wd_ref[lo:hi, :], preferred_element_type=F32)
        f = part if f is None else f + part
    return f


def _ffn_qkv_kernel(x_ref, gpre_ref, gpost_ref, wg_ref, wu_ref, wd_ref,
                    gmix_ref, win_ref, h_ref, qkv_ref):
    x = x_ref[...]
    u = _rms(x, gpre_ref[...]).astype(BF16)
    f = _swiglu(u, wg_ref, wu_ref, wd_ref)
    h = x + 0.5 * _rms(f, gpost_ref[...])
    h_ref[...] = h
    u2 = _rms(h, gmix_ref[...]).astype(BF16)
    qkv_ref[...] = jnp.dot(u2, win_ref[...], preferred_element_type=F32).astype(BF16)


def _resident(shape):
    nd = len(shape)
    return pl.BlockSpec(shape, lambda *_: (0,) * nd, pipeline_mode=pl.Buffered(1))


def _ffn_qkv(x2, gpre, gpost, wg, wu, wd, gmix, win):
    n, d = x2.shape
    d_ff = wg.shape[1]
    d_qkv = win.shape[1]
    row = lambda w: pl.BlockSpec((ROW_TILE, w), lambda i: (i, 0))
    return pl.pallas_call(
        _ffn_qkv_kernel,
        grid=(n // ROW_TILE,),
        in_specs=[row(d), _resident((1, d)), _resident((1, d)),
                  _resident((d, d_ff)), _resident((d, d_ff)), _resident((d_ff, d)),
                  _resident((1, d)), _resident((d, d_qkv))],
        out_specs=[row(d), row(d_qkv)],
        out_shape=[jax.ShapeDtypeStruct((n, d), F32),
                   jax.ShapeDtypeStruct((n, d_qkv), BF16)],
        compiler_params=pltpu.CompilerParams(
            dimension_semantics=("arbitrary",), vmem_limit_bytes=DENSE_VMEM_BYTES),
        name="ffn1_qkv",
    )(x2, gpre, gpost, wg, wu, wd, gmix, win)


def _split_heads(q2):
    lane = lax.broadcasted_iota(jnp.int32, q2.shape, 1)
    zero = jnp.zeros_like(q2)
    qs = q2 * jnp.asarray(SCALE, q2.dtype)
    return jnp.where(lane < HEAD_DIM, qs, zero), jnp.where(lane >= HEAD_DIM, qs, zero)


def _sb_kernel(q_ref, k_ref, v_ref, o_ref, carry_ref):
    t = SB_TILE
    npair = W_SB // LANES
    qi = pl.program_id(1)

    row = lax.broadcasted_iota(jnp.int32, (t, t), 0)
    col = lax.broadcasted_iota(jnp.int32, (t, t), 1)
    neg_upper = jnp.where(row >= col, -1.0, 0.0).astype(BF16)
    before = col < row
    first_head = lax.broadcasted_iota(jnp.int32, (t, LANES), 1) < HEAD_DIM

    def block(qh, kblk, vblk, diag):
        z = lax.dot_general(qh, kblk, (((1,), (1,)), ((), ())), preferred_element_type=F32)
        sp = jnp.maximum(z, 0.0) + jnp.log(1.0 + jnp.exp(-jnp.abs(z)))
        if diag:
            sp = jnp.where(before, sp, 0.0)
        sp_hi = sp.astype(BF16)
        sp_lo = (sp - sp_hi.astype(F32)).astype(BF16)
        c = (jnp.dot(sp_hi, neg_upper, preferred_element_type=F32)
             + jnp.dot(sp_lo, neg_upper, preferred_element_type=F32))
        p = jnp.exp(z + c)
        if diag:
            p = jnp.where(before, p, 0.0)
        pv = jnp.dot(p.astype(BF16), vblk, preferred_element_type=F32)
        return pv, c[:, 0:1]

    def sweep(kb, diag):
        start = pl.multiple_of(kb * t, t)
        worst = None
        for hp in range(npair):
            cols = slice(hp * LANES, (hp + 1) * LANES)
            kblk = k_ref[0, pl.ds(start, t), cols]
            vblk = v_ref[0, pl.ds(start, t), cols]
            res = [block(qh, kblk, vblk, diag) for qh in _split_heads(q_ref[0, :, cols])]
            if diag:
                o_ref[0, :, cols] = jnp.where(first_head, res[0][0], res[1][0])
                new = [res[0][1], res[1][1]]
            else:
                old = [carry_ref[2 * hp], carry_ref[2 * hp + 1]]
                scaled = jnp.where(first_head, jnp.exp(old[0]) * res[0][0],
                                   jnp.exp(old[1]) * res[1][0])
                o_ref[0, :, cols] += scaled
                new = [old[0] + res[0][1], old[1] + res[1][1]]
            for h in range(2):
                carry_ref[2 * hp + h] = new[h]
                worst = new[h] if worst is None else jnp.maximum(worst, new[h])
        return jnp.max(worst)

    sweep(qi, True)

    def cond(st):
        return jnp.logical_and(st[0] < qi, st[1] >= EXP_ZERO_BELOW)

    def body(st):
        return st[0] + 1, sweep(qi - 1 - st[0], False)

    lax.while_loop(cond, body, (jnp.int32(0), jnp.float32(0.0)))


def _sb_attention(qkv, b, s):
    t = SB_TILE
    return pl.pallas_call(
        _sb_kernel,
        grid=(b, s // t),
        in_specs=[pl.BlockSpec((1, t, W_SB), lambda bi, qi: (bi, qi, 0)),
                  pl.BlockSpec((1, s, W_SB), lambda bi, qi: (bi, 0, 1)),
                  pl.BlockSpec((1, s, W_SB), lambda bi, qi: (bi, 0, 2))],
        out_specs=pl.BlockSpec((1, t, W_SB), lambda bi, qi: (bi, qi, 0)),
        out_shape=jax.ShapeDtypeStruct((b, s, W_SB), F32),
        scratch_shapes=[pltpu.VMEM((H_SB, t, 1), F32)],
        compiler_params=pltpu.CompilerParams(
            dimension_semantics=("arbitrary", "arbitrary"),
            vmem_limit_bytes=ATTN_VMEM_BYTES),
        name="stickbreak_attn",
    )(qkv, qkv, qkv)


def _band_bias_table(rel_bias):
    width = CH_TAB_BLKS * LANES
    h = rel_bias.shape[0]
    rb = rel_bias.astype(F32)
    far = LOOKBACK * CHUNK + CH_TQ - 1 - REL_CLIP
    near = width + CH_TQ - 1 - far - (2 * REL_CLIP + 1)
    ext = jnp.concatenate([jnp.broadcast_to(rb[:, -1:], (h, far)), rb[:, ::-1],
                           jnp.broadcast_to(rb[:, :1], (h, near))], axis=1)
    rows = [ext[:, CH_TQ - 1 - i: CH_TQ - 1 - i + width] for i in range(CH_TQ)]
    tab = jnp.stack(rows, axis=1)
    i = np.arange(CH_TQ)[:, None]
    j = np.arange(width)[None, :]
    dchunk = j // CHUNK - i // CHUNK
    valid = (dchunk >= 0) & (dchunk <= LOOKBACK) & (j < CH_WIN)
    tab = jnp.where(jnp.asarray(valid)[None], tab, NEG_INF)
    return tab.reshape(h, CH_TQ, CH_TAB_BLKS, LANES).transpose(0, 2, 1, 3)


def _ch_kernel(q_ref, k_ref, v_ref, tab_ref, o_ref):
    qi = pl.program_id(1)
    npair = W_CH // LANES
    nominal = qi * CH_TQ - LOOKBACK * CHUNK
    start = pl.multiple_of(jnp.maximum(nominal, 0), LANES)
    shift_blk = (start - nominal) // LANES
    first_head = lax.broadcasted_iota(jnp.int32, (CH_TQ, LANES), 1) < HEAD_DIM
    for hp in range(npair):
        cols = slice(hp * LANES, (hp + 1) * LANES)
        kwin = k_ref[0, pl.ds(start, CH_WIN), cols]
        vwin = v_ref[0, pl.ds(start, CH_WIN), cols]
        outs = []
        for h, qh in enumerate(_split_heads(q_ref[0, :, cols])):
            bias = jnp.concatenate(
                [tab_ref[2 * hp + h, shift_blk + o] for o in range(CH_NBLK)], axis=-1)
            z = lax.dot_general(qh, kwin, (((1,), (1,)), ((), ())),
                                preferred_element_type=F32) + bias
            m = jnp.max(z, axis=-1, keepdims=True)
            p = jnp.exp(z - m)
            l = jnp.sum(p, axis=-1, keepdims=True)
            pv = jnp.dot(p.astype(BF16), vwin, preferred_element_type=F32)
            outs.append(pv / l)
        o_ref[0, :, cols] = jnp.where(first_head, outs[0], outs[1])


def _ch_attention(qkv, tab, b, s):
    base = 3 * W_SB // W_CH
    return pl.pallas_call(
        _ch_kernel,
        grid=(b, s // CH_TQ),
        in_specs=[pl.BlockSpec((1, CH_TQ, W_CH), lambda bi, qi: (bi, qi, base)),
                  pl.BlockSpec((1, s, W_CH), lambda bi, qi: (bi, 0, base + 1)),
                  pl.BlockSpec((1, s, W_CH), lambda bi, qi: (bi, 0, base + 2)),
                  _resident(tab.shape)],
        out_specs=pl.BlockSpec((1, CH_TQ, W_CH), lambda bi, qi: (bi, qi, 0)),
        out_shape=jax.ShapeDtypeStruct((b, s, W_CH), F32),
        compiler_params=pltpu.CompilerParams(
            dimension_semantics=("arbitrary", "arbitrary"),
            vmem_limit_bytes=ATTN_VMEM_BYTES),
        name="band_attn",
    )(qkv, qkv, qkv, tab)


def _tail_kernel(h_ref, oa_ref, ob_ref, p_ref, gsb_ref, gch_ref, wout_ref, gmixpost_ref,
                 gpre_ref, gpost_ref, wg_ref, wu_ref, wd_ref, wple_ref, wgate_ref, gple_ref,
                 out_ref):
    mixed = jnp.concatenate([_rms(oa_ref[...], gsb_ref[...]).astype(BF16),
                             _rms(ob_ref[...], gch_ref[...]).astype(BF16)], axis=-1)
    y = jnp.dot(mixed, wout_ref[...], preferred_element_type=F32)
    h = h_ref[...] + _rms(y, gmixpost_ref[...])
    u = _rms(h, gpre_ref[...]).astype(BF16)
    f = _swiglu(u, wg_ref, wu_ref, wd_ref)
    h = h + 0.5 * _rms(f, gpost_ref[...])
    proj = jnp.dot(p_ref[...].astype(BF16), wple_ref[...], preferred_element_type=F32)
    gate = jnp.dot(h.astype(BF16), wgate_ref[...], preferred_element_type=F32)
    e = proj * jax.nn.sigmoid(gate)
    out_ref[...] = h + _rms(e, gple_ref[...])


def _tail(h, oa, ob, p2, gsb, gch, wout, gmixpost, gpre, gpost, wg, wu, wd, wple, wgate, gple):
    n, d = h.shape
    d_ff = wg.shape[1]
    row = lambda w: pl.BlockSpec((ROW_TILE, w), lambda i: (i, 0))
    return pl.pallas_call(
        _tail_kernel,
        grid=(n // ROW_TILE,),
        in_specs=[row(d), row(oa.shape[1]), row(ob.shape[1]), row(p2.shape[1]),
                  _resident((1, oa.shape[1])), _resident((1, ob.shape[1])),
                  _resident(wout.shape), _resident((1, d)),
                  _resident((1, d)), _resident((1, d)),
                  _resident((d, d_ff)), _resident((d, d_ff)), _resident((d_ff, d)),
                  _resident(wple.shape), _resident(wgate.shape), _resident((1, d))],
        out_specs=row(d),
        out_shape=jax.ShapeDtypeStruct((n, d), F32),
        compiler_params=pltpu.CompilerParams(
            dimension_semantics=("arbitrary",), vmem_limit_bytes=DENSE_VMEM_BYTES),
        name="outproj_ffn2_ple",
    )(h, oa, ob, p2, gsb, gch, wout, gmixpost, gpre, gpost, wg, wu, wd, wple, wgate, gple)


def kernel(x, p, g_ffn1_pre, g_ffn1_post, w_ffn1_gate, w_ffn1_up, w_ffn1_down, g_mix_pre, g_mix_post, w_in, g_out_sb, g_out_ch, rel_bias, w_out, g_ffn2_pre, g_ffn2_post, w_ffn2_gate, w_ffn2_up, w_ffn2_down, w_ple_proj, w_ple_gate, g_ple_post):
    b, s, d = x.shape
    depth = p.shape[0]
    n = b * s
    gain = lambda g: g.astype(F32).reshape(1, -1)
    wcast = lambda w: w.astype(BF16)
    h = x.reshape(n, d)
    for i in range(depth):
        h1, qkv = _ffn_qkv(h, gain(g_ffn1_pre[i]), gain(g_ffn1_post[i]),
                           wcast(w_ffn1_gate[i]), wcast(w_ffn1_up[i]), wcast(w_ffn1_down[i]),
                           gain(g_mix_pre[i]), wcast(w_in[i]))
        qkv3 = qkv.reshape(b, s, qkv.shape[1])
        o_a = _sb_attention(qkv3, b, s).reshape(n, W_SB)
        o_b = _ch_attention(qkv3, _band_bias_table(rel_bias[i]), b, s).reshape(n, W_CH)
        h = _tail(h1, o_a, o_b, p[i].reshape(n, -1),
                  gain(g_out_sb[i]), gain(g_out_ch[i]), wcast(w_out[i]), gain(g_mix_post[i]),
                  gain(g_ffn2_pre[i]), gain(g_ffn2_post[i]),
                  wcast(w_ffn2_gate[i]), wcast(w_ffn2_up[i]), wcast(w_ffn2_down[i]),
                  wcast(w_ple_proj[i]), wcast(w_ple_gate[i]), gain(g_ple_post[i]))
    return h.reshape(b, s, d)
```

```python
import jax
import jax.numpy as jnp
import numpy as np
from jax import lax
from jax.experimental import pallas as pl
from jax.experimental.pallas import tpu as pltpu

HEAD_DIM = 64
H_SB = 8
H_CH = 8
W_SB = H_SB * HEAD_DIM
W_CH = H_CH * HEAD_DIM
CHUNK = 64
LOOKBACK = 8
REL_CLIP = 128
EPS = 1e-6
NEG_INF = -1e30
SCALE = HEAD_DIM ** -0.5

LANES = 128
ROW_TILE = 512
FF_SPLIT = 2
SB_TILE = 256
CH_TQ = 128
CH_WIN = LOOKBACK * CHUNK + CH_TQ
CH_NBLK = CH_WIN // LANES
CH_TAB_BLKS = CH_NBLK + LOOKBACK * CHUNK // LANES
EXP_ZERO_BELOW = -104.0
DENSE_VMEM_BYTES = 58 * 1024 * 1024
ATTN_VMEM_BYTES = 48 * 1024 * 1024

F32 = jnp.float32
BF16 = jnp.bfloat16


def _rms(x, g):
    ms = jnp.mean(x * x, axis=-1, keepdims=True)
    return x * lax.rsqrt(ms + EPS) * g


def _swiglu(u, wg_ref, wu_ref, wd_ref):
    d_ff = wg_ref.shape[1]
    slab = d_ff // FF_SPLIT
    f = None
    for c in range(FF_SPLIT):
        lo, hi = c * slab, (c + 1) * slab
        gate = jnp.dot(u, wg_ref[:, lo:hi], preferred_element_type=F32)
        up = jnp.dot(u, wu_ref[:, lo:hi], preferred_element_type=F32)
        act = (gate * jax.nn.sigmoid(gate) * up).astype(BF16)
        part = jnp.dot(act, wd_ref[lo:hi, :], preferred_element_type=F32)
        f = part if f is None else f + part
    return f


def _ffn_qkv_kernel(x_ref, gpre_ref, gpost_ref, wg_ref, wu_ref, wd_ref,
                    gmix_ref, win_ref, h_ref, qkv_ref):
    x = x_ref[...]
    u = _rms(x, gpre_ref[...]).astype(BF16)
    f = _swiglu(u, wg_ref, wu_ref, wd_ref)
    h = x + 0.5 * _rms(f, gpost_ref[...])
    h_ref[...] = h
    u2 = _rms(h, gmix_ref[...]).astype(BF16)
    qkv_ref[...] = jnp.dot(u2, win_ref[...], preferred_element_type=F32).astype(BF16)


def _resident(shape):
    nd = len(shape)
    return pl.BlockSpec(shape, lambda *_: (0,) * nd, pipeline_mode=pl.Buffered(1))


def _ffn_qkv(x2, gpre, gpost, wg, wu, wd, gmix, win):
    n, d = x2.shape
    d_ff = wg.shape[1]
    d_qkv = win.shape[1]
    row = lambda w: pl.BlockSpec((ROW_TILE, w), lambda i: (i, 0))
    return pl.pallas_call(
        _ffn_qkv_kernel,
        grid=(n // ROW_TILE,),
        in_specs=[row(d), _resident((1, d)), _resident((1, d)),
                  _resident((d, d_ff)), _resident((d, d_ff)), _resident((d_ff, d)),
                  _resident((1, d)), _resident((d, d_qkv))],
        out_specs=[row(d), row(d_qkv)],
        out_shape=[jax.ShapeDtypeStruct((n, d), F32),
                   jax.ShapeDtypeStruct((n, d_qkv), BF16)],
        compiler_params=pltpu.CompilerParams(
            dimension_semantics=("arbitrary",), vmem_limit_bytes=DENSE_VMEM_BYTES),
        name="ffn1_qkv",
    )(x2, gpre, gpost, wg, wu, wd, gmix, win)


def _stack_heads(q2):
    lane = lax.broadcasted_iota(jnp.int32, q2.shape, 1)
    zero = jnp.zeros_like(q2)
    qs = q2 * jnp.asarray(SCALE, q2.dtype)
    return jnp.concatenate([jnp.where(lane < HEAD_DIM, qs, zero),
                            jnp.where(lane >= HEAD_DIM, qs, zero)], axis=0)


_NT = (((1,), (1,)), ((), ()))


def _sb_kernel(q_ref, k_ref, v_ref, o_ref, carry_ref):
    t = SB_TILE
    npair = W_SB // LANES
    qi = pl.program_id(1)
    pair_cols = [slice(hp * LANES, (hp + 1) * LANES) for hp in range(npair)]

    row = lax.broadcasted_iota(jnp.int32, (2 * t, t), 0)
    col = lax.broadcasted_iota(jnp.int32, (2 * t, t), 1)
    neg_upper2 = jnp.where((row & (t - 1)) >= col, -1.0, 0.0).astype(BF16)
    before = col < (row & (t - 1))
    first_head = lax.broadcasted_iota(jnp.int32, (t, LANES), 1) < HEAD_DIM
    q_stacks = [_stack_heads(q_ref[0, :, c]) for c in pair_cols]

    def sweep(kb, diag):
        start = pl.multiple_of(kb * t, t)
        zs = [lax.dot_general(q, k_ref[0, pl.ds(start, t), c], _NT, preferred_element_type=F32)
              for q, c in zip(q_stacks, pair_cols)]
        lhs = []
        for z in zs:
            sp = jnp.maximum(z, 0.0) + jnp.log(1.0 + jnp.exp(-jnp.abs(z)))
            if diag:
                sp = jnp.where(before, sp, 0.0)
            sp_hi = sp.astype(BF16)
            sp_lo = (sp - sp_hi.astype(F32)).astype(BF16)
            lhs.append(jnp.concatenate([sp_hi, sp_lo], axis=1))
        cs = [jnp.dot(l, neg_upper2, preferred_element_type=F32) for l in lhs]
        ps = []
        for z, c in zip(zs, cs):
            p = jnp.exp(z + c)
            if diag:
                p = jnp.where(before, p, 0.0)
            ps.append(p.astype(BF16))
        pvs = [jnp.dot(p, v_ref[0, pl.ds(start, t), c], preferred_element_type=F32)
               for p, c in zip(ps, pair_cols)]
        worst = None
        for hp in range(npair):
            tot = cs[hp][:, 0:1]
            if diag:
                pv, new = pvs[hp], tot
                o_ref[0, :, pair_cols[hp]] = jnp.where(first_head, pv[:t], pv[t:])
            else:
                old = carry_ref[hp]
                pv, new = jnp.exp(old) * pvs[hp], old + tot
                o_ref[0, :, pair_cols[hp]] += jnp.where(first_head, pv[:t], pv[t:])
            carry_ref[hp] = new
            worst = new if worst is None else jnp.maximum(worst, new)
        return jnp.max(worst)

    sweep(qi, True)

    def cond(st):
        return jnp.logical_and(st[0] < qi, st[1] >= EXP_ZERO_BELOW)

    def body(st):
        return st[0] + 1, sweep(qi - 1 - st[0], False)

    lax.while_loop(cond, body, (jnp.int32(0), jnp.float32(0.0)))


def _sb_attention(qkv, b, s):
    t = SB_TILE
    return pl.pallas_call(
        _sb_kernel,
        grid=(b, s // t),
        in_specs=[pl.BlockSpec((1, t, W_SB), lambda bi, qi: (bi, qi, 0)),
                  pl.BlockSpec((1, s, W_SB), lambda bi, qi: (bi, 0, 1)),
                  pl.BlockSpec((1, s, W_SB), lambda bi, qi: (bi, 0, 2))],
        out_specs=pl.BlockSpec((1, t, W_SB), lambda bi, qi: (bi, qi, 0)),
        out_shape=jax.ShapeDtypeStruct((b, s, W_SB), F32),
        scratch_shapes=[pltpu.VMEM((W_SB // LANES, 2 * t, 1), F32)],
        compiler_params=pltpu.CompilerParams(
            dimension_semantics=("arbitrary", "arbitrary"),
            vmem_limit_bytes=ATTN_VMEM_BYTES),
        name="stickbreak_attn",
    )(qkv, qkv, qkv)


def _band_bias_table(rel_bias):
    width = CH_TAB_BLKS * LANES
    h = rel_bias.shape[0]
    rb = rel_bias.astype(F32)
    far = LOOKBACK * CHUNK + CH_TQ - 1 - REL_CLIP
    near = width + CH_TQ - 1 - far - (2 * REL_CLIP + 1)
    ext = jnp.concatenate([jnp.broadcast_to(rb[:, -1:], (h, far)), rb[:, ::-1],
                           jnp.broadcast_to(rb[:, :1], (h, near))], axis=1)
    rows = [ext[:, CH_TQ - 1 - i: CH_TQ - 1 - i + width] for i in range(CH_TQ)]
    tab = jnp.stack(rows, axis=1)
    i = np.arange(CH_TQ)[:, None]
    j = np.arange(width)[None, :]
    dchunk = j // CHUNK - i // CHUNK
    valid = (dchunk >= 0) & (dchunk <= LOOKBACK) & (j < CH_WIN)
    tab = jnp.where(jnp.asarray(valid)[None], tab, NEG_INF)
    return tab.reshape(h, CH_TQ, CH_TAB_BLKS, LANES).transpose(0, 2, 1, 3)


def _ch_kernel(q_ref, k_ref, v_ref, tab_ref, o_ref):
    qi = pl.program_id(1)
    npair = W_CH // LANES
    nominal = qi * CH_TQ - LOOKBACK * CHUNK
    start = pl.multiple_of(jnp.maximum(nominal, 0), LANES)
    shift_blk = (start - nominal) // LANES
    first_head = lax.broadcasted_iota(jnp.int32, (CH_TQ, LANES), 1) < HEAD_DIM
    pair_cols = [slice(hp * LANES, (hp + 1) * LANES) for hp in range(npair)]
    zs = [lax.dot_general(_stack_heads(q_ref[0, :, c]), k_ref[0, pl.ds(start, CH_WIN), c], _NT,
                          preferred_element_type=F32) for c in pair_cols]
    ps, ls = [], []
    for hp, z in enumerate(zs):
        bias = jnp.concatenate(
            [jnp.concatenate([tab_ref[2 * hp + h, shift_blk + o] for o in range(CH_NBLK)], axis=-1)
             for h in range(2)], axis=0)
        z = z + bias
        m = jnp.max(z, axis=-1, keepdims=True)
        p = jnp.exp(z - m)
        ls.append(jnp.sum(p, axis=-1, keepdims=True))
        ps.append(p.astype(BF16))
    pvs = [jnp.dot(p, v_ref[0, pl.ds(start, CH_WIN), c], preferred_element_type=F32)
           for p, c in zip(ps, pair_cols)]
    for hp in range(npair):
        out = pvs[hp] / ls[hp]
        o_ref[0, :, pair_cols[hp]] = jnp.where(first_head, out[:CH_TQ], out[CH_TQ:])


def _ch_attention(qkv, tab, b, s):
    base = 3 * W_SB // W_CH
    return pl.pallas_call(
        _ch_kernel,
        grid=(b, s // CH_TQ),
        in_specs=[pl.BlockSpec((1, CH_TQ, W_CH), lambda bi, qi: (bi, qi, base)),
                  pl.BlockSpec((1, s, W_CH), lambda bi, qi: (bi, 0, base + 1)),
                  pl.BlockSpec((1, s, W_CH), lambda bi, qi: (bi, 0, base + 2)),
                  _resident(tab.shape)],
        out_specs=pl.BlockSpec((1, CH_TQ, W_CH), lambda bi, qi: (bi, qi, 0)),
        out_shape=jax.ShapeDtypeStruct((b, s, W_CH), F32),
        compiler_params=pltpu.CompilerParams(
            dimension_semantics=("arbitrary", "arbitrary"),
            vmem_limit_bytes=ATTN_VMEM_BYTES),
        name="band_attn",
    )(qkv, qkv, qkv, tab)


def _tail_kernel(h_ref, oa_ref, ob_ref, p_ref, gsb_ref, gch_ref, wout_ref, gmixpost_ref,
                 gpre_ref, gpost_ref, wg_ref, wu_ref, wd_ref, wple_ref, wgate_ref, gple_ref,
                 out_ref):
    mixed = jnp.concatenate([_rms(oa_ref[...], gsb_ref[...]).astype(BF16),
                             _rms(ob_ref[...], gch_ref[...]).astype(BF16)], axis=-1)
    y = jnp.dot(mixed, wout_ref[...], preferred_element_type=F32)
    h = h_ref[...] + _rms(y, gmixpost_ref[...])
    u = _rms(h, gpre_ref[...]).astype(BF16)
    f = _swiglu(u, wg_ref, wu_ref, wd_ref)
    h = h + 0.5 * _rms(f, gpost_ref[...])
    proj = jnp.dot(p_ref[...].astype(BF16), wple_ref[...], preferred_element_type=F32)
    gate = jnp.dot(h.astype(BF16), wgate_ref[...], preferred_element_type=F32)
    e = proj * jax.nn.sigmoid(gate)
    out_ref[...] = h + _rms(e, gple_ref[...])


def _tail(h, oa, ob, p2, gsb, gch, wout, gmixpost, gpre, gpost, wg, wu, wd, wple, wgate, gple):
    n, d = h.shape
    d_ff = wg.shape[1]
    row = lambda w: pl.BlockSpec((ROW_TILE, w), lambda i: (i, 0))
    return pl.pallas_call(
        _tail_kernel,
        grid=(n // ROW_TILE,),
        in_specs=[row(d), row(oa.shape[1]), row(ob.shape[1]), row(p2.shape[1]),
                  _resident((1, oa.shape[1])), _resident((1, ob.shape[1])),
                  _resident(wout.shape), _resident((1, d)),
                  _resident((1, d)), _resident((1, d)),
                  _resident((d, d_ff)), _resident((d, d_ff)), _resident((d_ff, d)),
                  _resident(wple.shape), _resident(wgate.shape), _resident((1, d))],
        out_specs=row(d),
        out_shape=jax.ShapeDtypeStruct((n, d), F32),
        compiler_params=pltpu.CompilerParams(
            dimension_semantics=("arbitrary",), vmem_limit_bytes=DENSE_VMEM_BYTES),
        name="outproj_ffn2_ple",
    )(h, oa, ob, p2, gsb, gch, wout, gmixpost, gpre, gpost, wg, wu, wd, wple, wgate, gple)


def kernel(x, p, g_ffn1_pre, g_ffn1_post, w_ffn1_gate, w_ffn1_up, w_ffn1_down, g_mix_pre, g_mix_post, w_in, g_out_sb, g_out_ch, rel_bias, w_out, g_ffn2_pre, g_ffn2_post, w_ffn2_gate, w_ffn2_up, w_ffn2_down, w_ple_proj, w_ple_gate, g_ple_post):
    b, s, d = x.shape
    depth = p.shape[0]
    n = b * s
    gain = lambda g: g.astype(F32).reshape(1, -1)
    wcast = lambda w: w.astype(BF16)
    h = x.reshape(n, d)
    for i in range(depth):
        h1, qkv = _ffn_qkv(h, gain(g_ffn1_pre[i]), gain(g_ffn1_post[i]),
                           wcast(w_ffn1_gate[i]), wcast(w_ffn1_up[i]), wcast(w_ffn1_down[i]),
                           gain(g_mix_pre[i]), wcast(w_in[i]))
        qkv3 = qkv.reshape(b, s, qkv.shape[1])
        o_a = _sb_attention(qkv3, b, s).reshape(n, W_SB)
        o_b = _ch_attention(qkv3, _band_bias_table(rel_bias[i]), b, s).reshape(n, W_CH)
        h = _tail(h1, o_a, o_b, p[i].reshape(n, -1),
                  gain(g_out_sb[i]), gain(g_out_ch[i]), wcast(w_out[i]), gain(g_mix_post[i]),
                  gain(g_ffn2_pre[i]), gain(g_ffn2_post[i]),
                  wcast(w_ffn2_gate[i]), wcast(w_ffn2_up[i]), wcast(w_ffn2_down[i]),
                  wcast(w_ple_proj[i]), wcast(w_ple_gate[i]), gain(g_ple_post[i]))
    return h.reshape(b, s, d)
```

```python
import jax
import jax.numpy as jnp
import numpy as np
from jax import lax
from jax.experimental import pallas as pl
from jax.experimental.pallas import tpu as pltpu

HEAD_DIM = 64
H_SB = 8
H_CH = 8
W_SB = H_SB * HEAD_DIM
W_CH = H_CH * HEAD_DIM
CHUNK = 64
LOOKBACK = 8
REL_CLIP = 128
EPS = 1e-6
NEG_INF = -1e30
SCALE = HEAD_DIM ** -0.5

LANES = 128
MXU_DIM = 256
ROW_TILE = 512
FF_SPLIT = 2
SB_TILE = 256
CH_TQ = 128
CH_WIN = LOOKBACK * CHUNK + CH_TQ
CH_NBLK = CH_WIN // LANES
CH_TAB_BLKS = CH_NBLK + LOOKBACK * CHUNK // LANES
EXP_ZERO_BELOW = -104.0
DENSE_VMEM_BYTES = 58 * 1024 * 1024
ATTN_VMEM_BYTES = 48 * 1024 * 1024

F32 = jnp.float32
BF16 = jnp.bfloat16


def _rms(x, g):
    ms = jnp.mean(x * x, axis=-1, keepdims=True)
    return x * lax.rsqrt(ms + EPS) * g


def _swiglu(u, wg_ref, wu_ref, wd_ref):
    d_ff = wg_ref.shape[1]
    tiles = d_ff // MXU_DIM
    edges = [((c * tiles + FF_SPLIT - 1) // FF_SPLIT) * MXU_DIM for c in range(FF_SPLIT)] + [d_ff]
    f = None
    for lo, hi in zip(edges[:-1], edges[1:]):
        gate = jnp.dot(u, wg_ref[:, lo:hi], preferred_element_type=F32)
        up = jnp.dot(u, wu_ref[:, lo:hi], preferred_element_type=F32)
        act = (gate * jax.nn.sigmoid(gate) * up).astype(BF16)
        part = jnp.dot(act, wd_ref[lo:hi, :], preferred_element_type=F32)
        f = part if f is None else f + part
    return f


def _ffn_qkv_kernel(x_ref, gpre_ref, gpost_ref, wg_ref, wu_ref, wd_ref,
                    gmix_ref, win_ref, h_ref, qkv_ref):
    x = x_ref[...]
    u = _rms(x, gpre_ref[...]).astype(BF16)
    f = _swiglu(u, wg_ref, wu_ref, wd_ref)
    h = x + 0.5 * _rms(f, gpost_ref[...])
    h_ref[...] = h
    u2 = _rms(h, gmix_ref[...]).astype(BF16)
    qkv_ref[...] = jnp.dot(u2, win_ref[...], preferred_element_type=F32).astype(BF16)


def _resident(shape):
    nd = len(shape)
    return pl.BlockSpec(shape, lambda *_: (0,) * nd, pipeline_mode=pl.Buffered(1))


def _ffn_qkv(x2, gpre, gpost, wg, wu, wd, gmix, win):
    n, d = x2.shape
    d_ff = wg.shape[1]
    d_qkv = win.shape[1]
    row = lambda w: pl.BlockSpec((ROW_TILE, w), lambda i: (i, 0))
    return pl.pallas_call(
        _ffn_qkv_kernel,
        grid=(n // ROW_TILE,),
        in_specs=[row(d), _resident((1, d)), _resident((1, d)),
                  _resident((d, d_ff)), _resident((d, d_ff)), _resident((d_ff, d)),
                  _resident((1, d)), _resident((d, d_qkv))],
        out_specs=[row(d), row(d_qkv)],
        out_shape=[jax.ShapeDtypeStruct((n, d), F32),
                   jax.ShapeDtypeStruct((n, d_qkv), BF16)],
        compiler_params=pltpu.CompilerParams(
            dimension_semantics=("arbitrary",), vmem_limit_bytes=DENSE_VMEM_BYTES),
        name="ffn1_qkv",
    )(x2, gpre, gpost, wg, wu, wd, gmix, win)


def _stack_heads(q2):
    lane = lax.broadcasted_iota(jnp.int32, q2.shape, 1)
    zero = jnp.zeros_like(q2)
    qs = q2 * jnp.asarray(SCALE, q2.dtype)
    return jnp.concatenate([jnp.where(lane < HEAD_DIM, qs, zero),
                            jnp.where(lane >= HEAD_DIM, qs, zero)], axis=0)


_NT = (((1,), (1,)), ((), ()))


def _emit_skewed(n_units, stages):
    state = [None] * n_units
    for s in range(n_units + len(stages) - 1):
        for k in reversed(range(len(stages))):
            u = s - k
            if 0 <= u < n_units:
                state[u] = stages[k](u, state[u])
    return state


def _sb_kernel(q_ref, k_ref, v_ref, o_ref, carry_ref):
    t = SB_TILE
    npair = W_SB // LANES
    qi = pl.program_id(1)
    pair_cols = [slice(hp * LANES, (hp + 1) * LANES) for hp in range(npair)]

    row = lax.broadcasted_iota(jnp.int32, (2 * t, t), 0)
    col = lax.broadcasted_iota(jnp.int32, (2 * t, t), 1)
    neg_upper2 = jnp.where((row & (t - 1)) >= col, -1.0, 0.0).astype(BF16)
    before = col < (row & (t - 1))
    first_head = lax.broadcasted_iota(jnp.int32, (t, LANES), 1) < HEAD_DIM
    q_stacks = [_stack_heads(q_ref[0, :, c]) for c in pair_cols]

    def sweep(kb, diag):
        start = pl.multiple_of(kb * t, t)

        def scores(hp, _):
            return lax.dot_general(q_stacks[hp], k_ref[0, pl.ds(start, t), pair_cols[hp]], _NT,
                                   preferred_element_type=F32)

        def split(hp, z):
            neg_abs = lax.bitcast_convert_type(
                lax.bitcast_convert_type(z, jnp.uint32) | jnp.uint32(0x80000000), F32)
            sp = jnp.maximum(z, 0.0) + jnp.log(1.0 + jnp.exp(neg_abs))
            if diag:
                sp = jnp.where(before, sp, 0.0)
            sp_hi = sp.astype(BF16)
            sp_lo = (sp - sp_hi.astype(F32)).astype(BF16)
            return z, jnp.concatenate([sp_hi, sp_lo], axis=1)

        def cumsum(hp, st):
            return st[0], jnp.dot(st[1], neg_upper2, preferred_element_type=F32)

        def weights(hp, st):
            z, c = st
            p = jnp.exp(z + c)
            if diag:
                p = jnp.where(before, p, 0.0)
            return c[:, 0:1], p.astype(BF16)

        def values(hp, st):
            return st[0], jnp.dot(st[1], v_ref[0, pl.ds(start, t), pair_cols[hp]],
                                  preferred_element_type=F32)

        def update(hp, st):
            tot, pv = st
            if diag:
                new = tot
                o_ref[0, :, pair_cols[hp]] = jnp.where(first_head, pv[:t], pv[t:])
            else:
                old = carry_ref[hp]
                pv, new = jnp.exp(old) * pv, old + tot
                o_ref[0, :, pair_cols[hp]] += jnp.where(first_head, pv[:t], pv[t:])
            carry_ref[hp] = new
            return new

        carries = _emit_skewed(npair, [scores, split, cumsum, weights, values, update])
        worst = carries[0]
        for c in carries[1:]:
            worst = jnp.maximum(worst, c)
        return jnp.max(worst)

    sweep(qi, True)

    def cond(st):
        return jnp.logical_and(st[0] < qi, st[1] >= EXP_ZERO_BELOW)

    def body(st):
        return st[0] + 1, sweep(qi - 1 - st[0], False)

    lax.while_loop(cond, body, (jnp.int32(0), jnp.float32(0.0)))


def _sb_attention(qkv, b, s):
    t = SB_TILE
    return pl.pallas_call(
        _sb_kernel,
        grid=(b, s // t),
        in_specs=[pl.BlockSpec((1, t, W_SB), lambda bi, qi: (bi, qi, 0)),
                  pl.BlockSpec((1, s, W_SB), lambda bi, qi: (bi, 0, 1)),
                  pl.BlockSpec((1, s, W_SB), lambda bi, qi: (bi, 0, 2))],
        out_specs=pl.BlockSpec((1, t, W_SB), lambda bi, qi: (bi, qi, 0)),
        out_shape=jax.ShapeDtypeStruct((b, s, W_SB), F32),
        scratch_shapes=[pltpu.VMEM((W_SB // LANES, 2 * t, 1), F32)],
        compiler_params=pltpu.CompilerParams(
            dimension_semantics=("arbitrary", "arbitrary"),
            vmem_limit_bytes=ATTN_VMEM_BYTES),
        name="stickbreak_attn",
    )(qkv, qkv, qkv)


def _band_bias_table(rel_bias):
    width = CH_TAB_BLKS * LANES
    h = rel_bias.shape[0]
    rb = rel_bias.astype(F32)
    far = LOOKBACK * CHUNK + CH_TQ - 1 - REL_CLIP
    near = width + CH_TQ - 1 - far - (2 * REL_CLIP + 1)
    ext = jnp.concatenate([jnp.broadcast_to(rb[:, -1:], (h, far)), rb[:, ::-1],
                           jnp.broadcast_to(rb[:, :1], (h, near))], axis=1)
    rows = [ext[:, CH_TQ - 1 - i: CH_TQ - 1 - i + width] for i in range(CH_TQ)]
    tab = jnp.stack(rows, axis=1)
    i = np.arange(CH_TQ)[:, None]
    j = np.arange(width)[None, :]
    dchunk = j // CHUNK - i // CHUNK
    valid = (dchunk >= 0) & (dchunk <= LOOKBACK) & (j < CH_WIN)
    tab = jnp.where(jnp.asarray(valid)[None], tab, NEG_INF)
    return tab.reshape(h, CH_TQ, CH_TAB_BLKS, LANES).transpose(0, 2, 1, 3)


def _ch_kernel(q_ref, k_ref, v_ref, tab_ref, o_ref):
    qi = pl.program_id(1)
    npair = W_CH // LANES
    nominal = qi * CH_TQ - LOOKBACK * CHUNK
    start = pl.multiple_of(jnp.maximum(nominal, 0), LANES)
    shift_blk = (start - nominal) // LANES
    first_head = lax.broadcasted_iota(jnp.int32, (CH_TQ, LANES), 1) < HEAD_DIM
    pair_cols = [slice(hp * LANES, (hp + 1) * LANES) for hp in range(npair)]

    def scores(hp, _):
        c = pair_cols[hp]
        return lax.dot_general(_stack_heads(q_ref[0, :, c]), k_ref[0, pl.ds(start, CH_WIN), c],
                               _NT, preferred_element_type=F32)

    def softmax(hp, z):
        bias = jnp.concatenate(
            [jnp.concatenate([tab_ref[2 * hp + h, shift_blk + o] for o in range(CH_NBLK)], axis=-1)
             for h in range(2)], axis=0)
        z = z + bias
        m = jnp.max(z, axis=-1, keepdims=True)
        p = jnp.exp(z - m)
        return jnp.sum(p, axis=-1, keepdims=True), p.astype(BF16)

    def values(hp, st):
        return st[0], jnp.dot(st[1], v_ref[0, pl.ds(start, CH_WIN), pair_cols[hp]],
                              preferred_element_type=F32)

    def store(hp, st):
        out = st[1] / st[0]
        o_ref[0, :, pair_cols[hp]] = jnp.where(first_head, out[:CH_TQ], out[CH_TQ:])

    state = [None] * npair
    for stage in (scores, softmax, values, store):
        state = [stage(hp, st) for hp, st in enumerate(state)]


def _ch_attention(qkv, tab, b, s):
    base = 3 * W_SB // W_CH
    return pl.pallas_call(
        _ch_kernel,
        grid=(b, s // CH_TQ),
        in_specs=[pl.BlockSpec((1, CH_TQ, W_CH), lambda bi, qi: (bi, qi, base)),
                  pl.BlockSpec((1, s, W_CH), lambda bi, qi: (bi, 0, base + 1)),
                  pl.BlockSpec((1, s, W_CH), lambda bi, qi: (bi, 0, base + 2)),
                  _resident(tab.shape)],
        out_specs=pl.BlockSpec((1, CH_TQ, W_CH), lambda bi, qi: (bi, qi, 0)),
        out_shape=jax.ShapeDtypeStruct((b, s, W_CH), F32),
        compiler_params=pltpu.CompilerParams(
            dimension_semantics=("arbitrary", "arbitrary"),
            vmem_limit_bytes=ATTN_VMEM_BYTES),
        name="band_attn",
    )(qkv, qkv, qkv, tab)


def _tail_kernel(h_ref, oa_ref, ob_ref, p_ref, gsb_ref, gch_ref, wout_ref, gmixpost_ref,
                 gpre_ref, gpost_ref, wg_ref, wu_ref, wd_ref, wple_ref, wgate_ref, gple_ref,
                 out_ref):
    mixed = jnp.concatenate([_rms(oa_ref[...], gsb_ref[...]).astype(BF16),
                             _rms(ob_ref[...], gch_ref[...]).astype(BF16)], axis=-1)
    y = jnp.dot(mixed, wout_ref[...], preferred_element_type=F32)
    h = h_ref[...] + _rms(y, gmixpost_ref[...])
    u = _rms(h, gpre_ref[...]).astype(BF16)
    f = _swiglu(u, wg_ref, wu_ref, wd_ref)
    h = h + 0.5 * _rms(f, gpost_ref[...])
    proj = jnp.dot(p_ref[...].astype(BF16), wple_ref[...], preferred_element_type=F32)
    gate = jnp.dot(h.astype(BF16), wgate_ref[...], preferred_element_type=F32)
    e = proj * jax.nn.sigmoid(gate)
    out_ref[...] = h + _rms(e, gple_ref[...])


def _tail(h, oa, ob, p2, gsb, gch, wout, gmixpost, gpre, gpost, wg, wu, wd, wple, wgate, gple):
    n, d = h.shape
    d_ff = wg.shape[1]
    row = lambda w: pl.BlockSpec((ROW_TILE, w), lambda i: (i, 0))
    return pl.pallas_call(
        _tail_kernel,
        grid=(n // ROW_TILE,),
        in_specs=[row(d), row(oa.shape[1]), row(ob.shape[1]), row(p2.shape[1]),
                  _resident((1, oa.shape[1])), _resident((1, ob.shape[1])),
                  _resident(wout.shape), _resident((1, d)),
                  _resident((1, d)), _resident((1, d)),
                  _resident((d, d_ff)), _resident((d, d_ff)), _resident((d_ff, d)),
                  _resident(wple.shape), _resident(wgate.shape), _resident((1, d))],
        out_specs=row(d),
        out_shape=jax.ShapeDtypeStruct((n, d), F32),
        compiler_params=pltpu.CompilerParams(
            dimension_semantics=("arbitrary",), vmem_limit_bytes=DENSE_VMEM_BYTES),
        name="outproj_ffn2_ple",
    )(h, oa, ob, p2, gsb, gch, wout, gmixpost, gpre, gpost, wg, wu, wd, wple, wgate, gple)


def kernel(x, p, g_ffn1_pre, g_ffn1_post, w_ffn1_gate, w_ffn1_up, w_ffn1_down, g_mix_pre, g_mix_post, w_in, g_out_sb, g_out_ch, rel_bias, w_out, g_ffn2_pre, g_ffn2_post, w_ffn2_gate, w_ffn2_up, w_ffn2_down, w_ple_proj, w_ple_gate, g_ple_post):
    b, s, d = x.shape
    depth = p.shape[0]
    n = b * s
    gain = lambda g: g.astype(F32).reshape(1, -1)
    wcast = lambda w: w.astype(BF16)
    h = x.reshape(n, d)
    for i in range(depth):
        h1, qkv = _ffn_qkv(h, gain(g_ffn1_pre[i]), gain(g_ffn1_post[i]),
                           wcast(w_ffn1_gate[i]), wcast(w_ffn1_up[i]), wcast(w_ffn1_down[i]),
                           gain(g_mix_pre[i]), wcast(w_in[i]))
        qkv3 = qkv.reshape(b, s, qkv.shape[1])
        o_a = _sb_attention(qkv3, b, s).reshape(n, W_SB)
        o_b = _ch_attention(qkv3, _band_bias_table(rel_bias[i]), b, s).reshape(n, W_CH)
        h = _tail(h1, o_a, o_b, p[i].reshape(n, -1),
                  gain(g_out_sb[i]), gain(g_out_ch[i]), wcast(w_out[i]), gain(g_mix_post[i]),
                  gain(g_ffn2_pre[i]), gain(g_ffn2_post[i]),
                  wcast(w_ffn2_gate[i]), wcast(w_ffn2_up[i]), wcast(w_ffn2_down[i]),
                  wcast(w_ple_proj[i]), wcast(w_ple_gate[i]), gain(g_ple_post[i]))
    return h.reshape(b, s, d)
```

```python
import functools

import jax
import jax.numpy as jnp
from jax import lax
from jax.experimental import pallas as pl
from jax.experimental.pallas import tpu as pltpu

HEAD_DIM = 64
H_SB = 8
H_CH = 8
W_SB = H_SB * HEAD_DIM
W_CH = H_CH * HEAD_DIM
CHUNK = 64
LOOKBACK = 8
REL_CLIP = 128
EPS = 1e-6
NEG_INF = -1e30
SCALE = HEAD_DIM ** -0.5

LANES = 128
BF16_SUBLANES = 16
MXU_DIM = 256
ROW_TILE = 512
FF_SPLIT = 2
SB_TILE = 256
CH_TQ = 128
CH_WIN = LOOKBACK * CHUNK + CH_TQ
CH_NBLK = CH_WIN // LANES
CH_TAB_BLKS = CH_NBLK + LOOKBACK * CHUNK // LANES
CH_EXT = (CH_TAB_BLKS + 1) * LANES
CH_SUB = 2
EXP_ZERO_BELOW = -104.0
DENSE_VMEM_BYTES = 58 * 1024 * 1024
ATTN_VMEM_BYTES = 48 * 1024 * 1024

F32 = jnp.float32
BF16 = jnp.bfloat16


def _rms(x, g):
    ms = jnp.mean(x * x, axis=-1, keepdims=True)
    return x * lax.rsqrt(ms + EPS) * g


def _swiglu(u, wg_ref, wu_ref, wd_ref):
    d_ff = wg_ref.shape[1]
    tiles = d_ff // MXU_DIM
    edges = [((c * tiles + FF_SPLIT - 1) // FF_SPLIT) * MXU_DIM for c in range(FF_SPLIT)] + [d_ff]
    f = None
    for lo, hi in zip(edges[:-1], edges[1:]):
        gate = jnp.dot(u, wg_ref[:, lo:hi], preferred_element_type=F32)
        up = jnp.dot(u, wu_ref[:, lo:hi], preferred_element_type=F32)
        act = (gate * jax.nn.sigmoid(gate) * up).astype(BF16)
        part = jnp.dot(act, wd_ref[lo:hi, :], preferred_element_type=F32)
        f = part if f is None else f + part
    return f


def _ffn_qkv_kernel(n_cast, x_ref, gpre_ref, gpost_ref, wg_ref, wu_ref, wd_ref,
                    gmix_ref, win_ref, *rest):
    cast_in, (h_ref, qkv_ref), cast_out = rest[:n_cast], rest[n_cast:n_cast + 2], rest[n_cast + 2:]
    for src, dst in zip(cast_in, cast_out):
        dst[...] = src[...].astype(BF16)
    x = x_ref[...]
    u = _rms(x, gpre_ref[...]).astype(BF16)
    f = _swiglu(u, wg_ref, wu_ref, wd_ref)
    h = x + 0.5 * _rms(f, gpost_ref[...])
    h_ref[...] = h
    u2 = _rms(h, gmix_ref[...]).astype(BF16)
    qkv_ref[...] = jnp.dot(u2, win_ref[...], preferred_element_type=F32).astype(BF16)


def _resident(shape):
    nd = len(shape)
    return pl.BlockSpec(shape, lambda *_: (0,) * nd, pipeline_mode=pl.Buffered(1))


def _cast_block_spec(shape, n_steps):
    rows, cols = shape
    blk = next(r for r in range(BF16_SUBLANES, rows + 1, BF16_SUBLANES)
               if rows % r == 0 and r * n_steps >= rows)
    last = rows // blk - 1
    return pl.BlockSpec((blk, cols), lambda i: (jnp.minimum(i, last), 0))


def _ffn_qkv(x2, gpre, gpost, wg, wu, wd, gmix, win, later_weights):
    n, d = x2.shape
    d_ff = wg.shape[1]
    d_qkv = win.shape[1]
    n_steps = n // ROW_TILE
    row = lambda w: pl.BlockSpec((ROW_TILE, w), lambda i: (i, 0))
    cast_specs = [_cast_block_spec(w.shape, n_steps) for w in later_weights]
    outs = pl.pallas_call(
        functools.partial(_ffn_qkv_kernel, len(later_weights)),
        grid=(n_steps,),
        in_specs=[row(d), _resident((1, d)), _resident((1, d)),
                  _resident((d, d_ff)), _resident((d, d_ff)), _resident((d_ff, d)),
                  _resident((1, d)), _resident((d, d_qkv))] + cast_specs,
        out_specs=[row(d), row(d_qkv)] + cast_specs,
        out_shape=[jax.ShapeDtypeStruct((n, d), F32),
                   jax.ShapeDtypeStruct((n, d_qkv), BF16)]
                  + [jax.ShapeDtypeStruct(w.shape, BF16) for w in later_weights],
        compiler_params=pltpu.CompilerParams(
            dimension_semantics=("arbitrary",), vmem_limit_bytes=DENSE_VMEM_BYTES),
        name="ffn1_qkv",
    )(x2, gpre, gpost, wg, wu, wd, gmix, win, *later_weights)
    return outs[0], outs[1], outs[2:]


def _stack_heads(q2):
    lane = lax.broadcasted_iota(jnp.int32, q2.shape, 1)
    zero = jnp.zeros_like(q2)
    qs = q2 * jnp.asarray(SCALE, q2.dtype)
    return jnp.concatenate([jnp.where(lane < HEAD_DIM, qs, zero),
                            jnp.where(lane >= HEAD_DIM, qs, zero)], axis=0)


_NT = (((1,), (1,)), ((), ()))


def _emit_skewed(n_units, stages):
    state = [None] * n_units
    for s in range(n_units + len(stages) - 1):
        for k in reversed(range(len(stages))):
            u = s - k
            if 0 <= u < n_units:
                state[u] = stages[k](u, state[u])
    return state


def _sb_kernel(q_ref, k_ref, v_ref, o_ref, carry_ref):
    t = SB_TILE
    npair = W_SB // LANES
    qi = pl.program_id(1)
    pair_cols = [slice(hp * LANES, (hp + 1) * LANES) for hp in range(npair)]

    row = lax.broadcasted_iota(jnp.int32, (2 * t, t), 0)
    col = lax.broadcasted_iota(jnp.int32, (2 * t, t), 1)
    neg_upper2 = jnp.where((row & (t - 1)) >= col, -1.0, 0.0).astype(BF16)
    before = col < (row & (t - 1))
    first_head = lax.broadcasted_iota(jnp.int32, (t, LANES), 1) < HEAD_DIM
    q_stacks = [_stack_heads(q_ref[0, :, c]) for c in pair_cols]

    def sweep(kb, diag):
        start = pl.multiple_of(kb * t, t)

        def scores(hp, _):
            return lax.dot_general(q_stacks[hp], k_ref[0, pl.ds(start, t), pair_cols[hp]], _NT,
                                   preferred_element_type=F32)

        def split(hp, z):
            sp = jnp.maximum(z, 0.0) + jnp.log(1.0 + jnp.exp(-jnp.abs(z)))
            if diag:
                sp = jnp.where(before, sp, 0.0)
            sp_hi = sp.astype(BF16)
            sp_lo = (sp - sp_hi.astype(F32)).astype(BF16)
            return z, jnp.concatenate([sp_hi, sp_lo], axis=1)

        def cumsum(hp, st):
            return st[0], jnp.dot(st[1], neg_upper2, preferred_element_type=F32)

        def weights(hp, st):
            z, c = st
            p = jnp.exp(z + c)
            if diag:
                p = jnp.where(before, p, 0.0)
            return c[:, 0:1], p.astype(BF16)

        def values(hp, st):
            return st[0], jnp.dot(st[1], v_ref[0, pl.ds(start, t), pair_cols[hp]],
                                  preferred_element_type=F32)

        def update(hp, st):
            tot, pv = st
            if diag:
                new = tot
                o_ref[0, :, pair_cols[hp]] = jnp.where(first_head, pv[:t], pv[t:])
            else:
                old = carry_ref[hp]
                pv, new = jnp.exp(old) * pv, old + tot
                o_ref[0, :, pair_cols[hp]] += jnp.where(first_head, pv[:t], pv[t:])
            carry_ref[hp] = new
            return new

        carries = _emit_skewed(npair, [scores, split, cumsum, weights, values, update])
        worst = carries[0]
        for c in carries[1:]:
            worst = jnp.maximum(worst, c)
        return jnp.max(worst)

    sweep(qi, True)

    def cond(st):
        return jnp.logical_and(st[0] < qi, st[1] >= EXP_ZERO_BELOW)

    def body(st):
        return st[0] + 1, sweep(qi - 1 - st[0], False)

    lax.while_loop(cond, body, (jnp.int32(0), jnp.float32(0.0)))


def _sb_attention(qkv, b, s):
    t = SB_TILE
    return pl.pallas_call(
        _sb_kernel,
        grid=(b, s // t),
        in_specs=[pl.BlockSpec((1, t, W_SB), lambda bi, qi: (bi, qi, 0)),
                  pl.BlockSpec((1, s, W_SB), lambda bi, qi: (bi, 0, 1)),
                  pl.BlockSpec((1, s, W_SB), lambda bi, qi: (bi, 0, 2))],
        out_specs=pl.BlockSpec((1, t, W_SB), lambda bi, qi: (bi, qi, 0)),
        out_shape=jax.ShapeDtypeStruct((b, s, W_SB), F32),
        scratch_shapes=[pltpu.VMEM((W_SB // LANES, 2 * t, 1), F32)],
        compiler_params=pltpu.CompilerParams(
            dimension_semantics=("arbitrary", "arbitrary"),
            vmem_limit_bytes=ATTN_VMEM_BYTES),
        name="stickbreak_attn",
    )(qkv, qkv, qkv)


def _bias_by_offset(rel_bias):
    h = rel_bias.shape[0]
    rb = rel_bias.astype(F32)
    far = LOOKBACK * CHUNK + CH_TQ - 1 - REL_CLIP
    near = CH_EXT - far - (2 * REL_CLIP + 1)
    ext = jnp.concatenate([jnp.broadcast_to(rb[:, -1:], (h, far)), rb[:, ::-1],
                           jnp.broadcast_to(rb[:, :1], (h, near))], axis=1)
    return ext.reshape(h, 1, CH_EXT)


def _build_bias_table(ext_ref, tab_ref):
    shape = (CH_TQ, CH_EXT)
    i = lax.broadcasted_iota(jnp.int32, shape, 0)
    j = lax.broadcasted_iota(jnp.int32, shape, 1)
    dchunk = j // CHUNK - i // CHUNK
    valid = (dchunk >= 0) & (dchunk <= LOOKBACK) & (j < CH_WIN)
    for h in range(H_CH):
        rolled = pltpu.roll(jnp.broadcast_to(ext_ref[h], shape), CH_EXT - (CH_TQ - 1), 1,
                            stride=1, stride_axis=0)
        tab = jnp.where(valid, rolled, NEG_INF)
        for o in range(CH_TAB_BLKS):
            tab_ref[h, o] = tab[:, o * LANES:(o + 1) * LANES]


def _ch_kernel(ext_ref, q_ref, k_ref, v_ref, o_ref, tab_ref):
    @pl.when((pl.program_id(0) == 0) & (pl.program_id(1) == 0))
    def _():
        _build_bias_table(ext_ref, tab_ref)

    npair = W_CH // LANES
    first_head = lax.broadcasted_iota(jnp.int32, (CH_TQ, LANES), 1) < HEAD_DIM
    units = [(sub, hp) for sub in range(CH_SUB) for hp in range(npair)]
    rows = [slice(sub * CH_TQ, (sub + 1) * CH_TQ) for sub in range(CH_SUB)]
    cols = [slice(hp * LANES, (hp + 1) * LANES) for hp in range(npair)]
    starts, shifts = [], []
    for sub in range(CH_SUB):
        nominal = (pl.program_id(1) * CH_SUB + sub) * CH_TQ - LOOKBACK * CHUNK
        start = pl.multiple_of(jnp.maximum(nominal, 0), LANES)
        starts.append(start)
        shifts.append((start - nominal) // LANES)

    def scores(u, _):
        sub, hp = units[u]
        return lax.dot_general(_stack_heads(q_ref[0, rows[sub], cols[hp]]),
                               k_ref[0, pl.ds(starts[sub], CH_WIN), cols[hp]],
                               _NT, preferred_element_type=F32)

    def softmax(u, z):
        sub, hp = units[u]
        bias = jnp.concatenate(
            [jnp.concatenate([tab_ref[2 * hp + h, shifts[sub] + o] for o in range(CH_NBLK)],
                             axis=-1) for h in range(2)], axis=0)
        z = z + bias
        m = jnp.max(z, axis=-1, keepdims=True)
        p = jnp.exp(z - m)
        return jnp.sum(p, axis=-1, keepdims=True), p.astype(BF16)

    def values(u, st):
        sub, hp = units[u]
        return st[0], jnp.dot(st[1], v_ref[0, pl.ds(starts[sub], CH_WIN), cols[hp]],
                              preferred_element_type=F32)

    def store(u, st):
        sub, hp = units[u]
        out = st[1] / st[0]
        o_ref[0, rows[sub], cols[hp]] = jnp.where(first_head, out[:CH_TQ], out[CH_TQ:])

    state = [None] * len(units)
    for stage in (scores, softmax, values, store):
        state = [stage(u, st) for u, st in enumerate(state)]


def _ch_attention(qkv, ext, b, s):
    base = 3 * W_SB // W_CH
    tq = CH_SUB * CH_TQ
    return pl.pallas_call(
        _ch_kernel,
        grid=(b, s // tq),
        in_specs=[_resident(ext.shape),
                  pl.BlockSpec((1, tq, W_CH), lambda bi, qi: (bi, qi, base)),
                  pl.BlockSpec((1, s, W_CH), lambda bi, qi: (bi, 0, base + 1)),
                  pl.BlockSpec((1, s, W_CH), lambda bi, qi: (bi, 0, base + 2))],
        out_specs=pl.BlockSpec((1, tq, W_CH), lambda bi, qi: (bi, qi, 0)),
        out_shape=jax.ShapeDtypeStruct((b, s, W_CH), F32),
        scratch_shapes=[pltpu.VMEM((H_CH, CH_TAB_BLKS, CH_TQ, LANES), F32)],
        compiler_params=pltpu.CompilerParams(
            dimension_semantics=("arbitrary", "arbitrary"),
            vmem_limit_bytes=ATTN_VMEM_BYTES),
        name="band_attn",
    )(ext, qkv, qkv, qkv)


def _tail_kernel(h_ref, oa_ref, ob_ref, p_ref, gsb_ref, gch_ref, wout_ref, gmixpost_ref,
                 gpre_ref, gpost_ref, wg_ref, wu_ref, wd_ref, wple_ref, wgate_ref, gple_ref,
                 out_ref):
    mixed = jnp.concatenate([_rms(oa_ref[...], gsb_ref[...]).astype(BF16),
                             _rms(ob_ref[...], gch_ref[...]).astype(BF16)], axis=-1)
    y = jnp.dot(mixed, wout_ref[...], preferred_element_type=F32)
    h = h_ref[...] + _rms(y, gmixpost_ref[...])
    u = _rms(h, gpre_ref[...]).astype(BF16)
    f = _swiglu(u, wg_ref, wu_ref, wd_ref)
    h = h + 0.5 * _rms(f, gpost_ref[...])
    proj = jnp.dot(p_ref[...].astype(BF16), wple_ref[...], preferred_element_type=F32)
    gate = jnp.dot(h.astype(BF16), wgate_ref[...], preferred_element_type=F32)
    e = proj * jax.nn.sigmoid(gate)
    out_ref[...] = h + _rms(e, gple_ref[...])


def _tail(h, oa, ob, p2, gsb, gch, wout, gmixpost, gpre, gpost, wg, wu, wd, wple, wgate, gple):
    n, d = h.shape
    d_ff = wg.shape[1]
    row = lambda w: pl.BlockSpec((ROW_TILE, w), lambda i: (i, 0))
    return pl.pallas_call(
        _tail_kernel,
        grid=(n // ROW_TILE,),
        in_specs=[row(d), row(oa.shape[1]), row(ob.shape[1]), row(p2.shape[1]),
                  _resident((1, oa.shape[1])), _resident((1, ob.shape[1])),
                  _resident(wout.shape), _resident((1, d)),
                  _resident((1, d)), _resident((1, d)),
                  _resident((d, d_ff)), _resident((d, d_ff)), _resident((d_ff, d)),
                  _resident(wple.shape), _resident(wgate.shape), _resident((1, d))],
        out_specs=row(d),
        out_shape=jax.ShapeDtypeStruct((n, d), F32),
        compiler_params=pltpu.CompilerParams(
            dimension_semantics=("arbitrary",), vmem_limit_bytes=DENSE_VMEM_BYTES),
        name="outproj_ffn2_ple",
    )(h, oa, ob, p2, gsb, gch, wout, gmixpost, gpre, gpost, wg, wu, wd, wple, wgate, gple)


def kernel(x, p, g_ffn1_pre, g_ffn1_post, w_ffn1_gate, w_ffn1_up, w_ffn1_down, g_mix_pre, g_mix_post, w_in, g_out_sb, g_out_ch, rel_bias, w_out, g_ffn2_pre, g_ffn2_post, w_ffn2_gate, w_ffn2_up, w_ffn2_down, w_ple_proj, w_ple_gate, g_ple_post):
    b, s, d = x.shape
    depth = p.shape[0]
    n = b * s
    gain = lambda g: g.astype(F32).reshape(1, -1)
    wcast = lambda w: w.astype(BF16)
    h = x.reshape(n, d)
    for i in range(depth):
        later = [w_out[i], w_ffn2_gate[i], w_ffn2_up[i], w_ffn2_down[i],
                 w_ple_proj[i], w_ple_gate[i]]
        h1, qkv, later_bf16 = _ffn_qkv(
            h, gain(g_ffn1_pre[i]), gain(g_ffn1_post[i]),
            wcast(w_ffn1_gate[i]), wcast(w_ffn1_up[i]), wcast(w_ffn1_down[i]),
            gain(g_mix_pre[i]), wcast(w_in[i]), later)
        wout, wg2, wu2, wd2, wple, wgate = later_bf16
        qkv3 = qkv.reshape(b, s, qkv.shape[1])
        o_a = _sb_attention(qkv3, b, s).reshape(n, W_SB)
        o_b = _ch_attention(qkv3, _bias_by_offset(rel_bias[i]), b, s).reshape(n, W_CH)
        h = _tail(h1, o_a, o_b, p[i].reshape(n, -1),
                  gain(g_out_sb[i]), gain(g_out_ch[i]), wout, gain(g_mix_post[i]),
                  gain(g_ffn2_pre[i]), gain(g_ffn2_post[i]), wg2, wu2, wd2,
                  wple, wgate, gain(g_ple_post[i]))
    return h.reshape(b, s, d)
```

```python
import functools

import jax
import jax.numpy as jnp
from jax import lax
from jax.experimental import pallas as pl
from jax.experimental.pallas import tpu as pltpu

HEAD_DIM = 64
H_SB = 8
H_CH = 8
W_SB = H_SB * HEAD_DIM
W_CH = H_CH * HEAD_DIM
CHUNK = 64
LOOKBACK = 8
REL_CLIP = 128
EPS = 1e-6
NEG_INF = -1e30
SCALE = HEAD_DIM ** -0.5

LANES = 128
BF16_SUBLANES = 16
MXU_DIM = 256
ROW_TILE = 512
FF_SPLIT = 2
SB_TILE = 256
CH_TQ = 128
CH_WIN = LOOKBACK * CHUNK + CH_TQ
CH_NBLK = CH_WIN // LANES
CH_TAB_BLKS = CH_NBLK + LOOKBACK * CHUNK // LANES
CH_EXT = (CH_TAB_BLKS + 1) * LANES
CH_SUB = 2
EXP_ZERO_BELOW = -104.0
DENSE_VMEM_BYTES = 58 * 1024 * 1024
ATTN_VMEM_BYTES = 48 * 1024 * 1024

F32 = jnp.float32
BF16 = jnp.bfloat16


def _rms(x, g):
    ms = jnp.mean(x * x, axis=-1, keepdims=True)
    return x * lax.rsqrt(ms + EPS) * g


def _swiglu(u, wg_ref, wu_ref, wd_ref):
    d_ff = wg_ref.shape[1]
    tiles = d_ff // MXU_DIM
    edges = [((c * tiles + FF_SPLIT - 1) // FF_SPLIT) * MXU_DIM for c in range(FF_SPLIT)] + [d_ff]
    f = None
    for lo, hi in zip(edges[:-1], edges[1:]):
        gate = jnp.dot(u, wg_ref[:, lo:hi], preferred_element_type=F32)
        up = jnp.dot(u, wu_ref[:, lo:hi], preferred_element_type=F32)
        act = (gate * jax.nn.sigmoid(gate) * up).astype(BF16)
        part = jnp.dot(act, wd_ref[lo:hi, :], preferred_element_type=F32)
        f = part if f is None else f + part
    return f


def _ffn_qkv_kernel(n_cast, x_ref, gpre_ref, gpost_ref, wg_ref, wu_ref, wd_ref,
                    gmix_ref, win_ref, *rest):
    cast_in, (h_ref, qkv_ref), cast_out = rest[:n_cast], rest[n_cast:n_cast + 2], rest[n_cast + 2:]
    for src, dst in zip(cast_in, cast_out):
        dst[...] = src[...].astype(BF16)
    x = x_ref[...]
    u = _rms(x, gpre_ref[...]).astype(BF16)
    f = _swiglu(u, wg_ref, wu_ref, wd_ref)
    h = x + 0.5 * _rms(f, gpost_ref[...])
    h_ref[...] = h
    u2 = _rms(h, gmix_ref[...]).astype(BF16)
    qkv_ref[...] = jnp.dot(u2, win_ref[...], preferred_element_type=F32).astype(BF16)


def _resident(shape):
    nd = len(shape)
    return pl.BlockSpec(shape, lambda *_: (0,) * nd, pipeline_mode=pl.Buffered(1))


def _cast_block_spec(shape, n_steps):
    rows, cols = shape
    blk = next(r for r in range(BF16_SUBLANES, rows + 1, BF16_SUBLANES)
               if rows % r == 0 and r * n_steps >= rows)
    last = rows // blk - 1
    return pl.BlockSpec((blk, cols), lambda i: (jnp.minimum(i, last), 0))


def _ffn_qkv(x2, gpre, gpost, wg, wu, wd, gmix, win, later_weights):
    n, d = x2.shape
    d_ff = wg.shape[1]
    d_qkv = win.shape[1]
    n_steps = n // ROW_TILE
    row = lambda w: pl.BlockSpec((ROW_TILE, w), lambda i: (i, 0))
    cast_specs = [_cast_block_spec(w.shape, n_steps) for w in later_weights]
    outs = pl.pallas_call(
        functools.partial(_ffn_qkv_kernel, len(later_weights)),
        grid=(n_steps,),
        in_specs=[row(d), _resident((1, d)), _resident((1, d)),
                  _resident((d, d_ff)), _resident((d, d_ff)), _resident((d_ff, d)),
                  _resident((1, d)), _resident((d, d_qkv))] + cast_specs,
        out_specs=[row(d), row(d_qkv)] + cast_specs,
        out_shape=[jax.ShapeDtypeStruct((n, d), F32),
                   jax.ShapeDtypeStruct((n, d_qkv), BF16)]
                  + [jax.ShapeDtypeStruct(w.shape, BF16) for w in later_weights],
        compiler_params=pltpu.CompilerParams(
            dimension_semantics=("arbitrary",), vmem_limit_bytes=DENSE_VMEM_BYTES),
        name="ffn1_qkv",
    )(x2, gpre, gpost, wg, wu, wd, gmix, win, *later_weights)
    return outs[0], outs[1], outs[2:]


def _stack_heads(q2):
    lane = lax.broadcasted_iota(jnp.int32, q2.shape, 1)
    zero = jnp.zeros_like(q2)
    qs = q2 * jnp.asarray(SCALE, q2.dtype)
    return jnp.concatenate([jnp.where(lane < HEAD_DIM, qs, zero),
                            jnp.where(lane >= HEAD_DIM, qs, zero)], axis=0)


_NT = (((1,), (1,)), ((), ()))


def _emit_skewed(n_units, stages):
    state = [None] * n_units
    for s in range(n_units + len(stages) - 1):
        for k in reversed(range(len(stages))):
            u = s - k
            if 0 <= u < n_units:
                state[u] = stages[k](u, state[u])
    return state


def _sb_kernel(q_ref, k_ref, v_ref, o_ref, carry_ref):
    t = SB_TILE
    npair = W_SB // LANES
    qi = pl.program_id(1)
    pair_cols = [slice(hp * LANES, (hp + 1) * LANES) for hp in range(npair)]

    row = lax.broadcasted_iota(jnp.int32, (2 * t, t), 0)
    col = lax.broadcasted_iota(jnp.int32, (2 * t, t), 1)
    neg_after = jnp.where(lax.broadcasted_iota(jnp.int32, (t, t), 0)
                          > lax.broadcasted_iota(jnp.int32, (t, t), 1), -1.0, 0.0).astype(BF16)
    before = col < (row & (t - 1))
    first_head = lax.broadcasted_iota(jnp.int32, (t, LANES), 1) < HEAD_DIM
    q_stacks = [_stack_heads(q_ref[0, :, c]) for c in pair_cols]

    def sweep(kb, diag):
        start = pl.multiple_of(kb * t, t)

        def scores(hp, _):
            return lax.dot_general(q_stacks[hp], k_ref[0, pl.ds(start, t), pair_cols[hp]], _NT,
                                   preferred_element_type=F32)

        def split(hp, z):
            sp = jnp.maximum(z, 0.0) + jnp.log(1.0 + jnp.exp(-jnp.abs(z)))
            if diag:
                sp = jnp.where(before, sp, 0.0)
            return z - sp, sp[:, 0:1], sp.astype(BF16)

        def cumsum(hp, st):
            return st[0], st[1], jnp.dot(st[2], neg_after, preferred_element_type=F32)

        def weights(hp, st):
            log_sig, sp_first, c = st
            p = jnp.exp(log_sig + c)
            if diag:
                p = jnp.where(before, p, 0.0)
            return c[:, 0:1] - sp_first, p.astype(BF16)

        def values(hp, st):
            return st[0], jnp.dot(st[1], v_ref[0, pl.ds(start, t), pair_cols[hp]],
                                  preferred_element_type=F32)

        def update(hp, st):
            tot, pv = st
            if diag:
                new = tot
                o_ref[0, :, pair_cols[hp]] = jnp.where(first_head, pv[:t], pv[t:])
            else:
                old = carry_ref[hp]
                pv, new = jnp.exp(old) * pv, old + tot
                o_ref[0, :, pair_cols[hp]] += jnp.where(first_head, pv[:t], pv[t:])
            carry_ref[hp] = new
            return new

        carries = _emit_skewed(npair, [scores, split, cumsum, weights, values, update])
        worst = carries[0]
        for c in carries[1:]:
            worst = jnp.maximum(worst, c)
        return jnp.max(worst)

    sweep(qi, True)

    def cond(st):
        return jnp.logical_and(st[0] < qi, st[1] >= EXP_ZERO_BELOW)

    def body(st):
        return st[0] + 1, sweep(qi - 1 - st[0], False)

    lax.while_loop(cond, body, (jnp.int32(0), jnp.float32(0.0)))


def _sb_attention(qkv, b, s):
    t = SB_TILE
    return pl.pallas_call(
        _sb_kernel,
        grid=(b, s // t),
        in_specs=[pl.BlockSpec((1, t, W_SB), lambda bi, qi: (bi, qi, 0)),
                  pl.BlockSpec((1, s, W_SB), lambda bi, qi: (bi, 0, 1)),
                  pl.BlockSpec((1, s, W_SB), lambda bi, qi: (bi, 0, 2))],
        out_specs=pl.BlockSpec((1, t, W_SB), lambda bi, qi: (bi, qi, 0)),
        out_shape=jax.ShapeDtypeStruct((b, s, W_SB), F32),
        scratch_shapes=[pltpu.VMEM((W_SB // LANES, 2 * t, 1), F32)],
        compiler_params=pltpu.CompilerParams(
            dimension_semantics=("arbitrary", "arbitrary"),
            vmem_limit_bytes=ATTN_VMEM_BYTES),
        name="stickbreak_attn",
    )(qkv, qkv, qkv)


def _bias_by_offset(rel_bias):
    h = rel_bias.shape[0]
    rb = rel_bias.astype(F32)
    far = LOOKBACK * CHUNK + CH_TQ - 1 - REL_CLIP
    near = CH_EXT - far - (2 * REL_CLIP + 1)
    ext = jnp.concatenate([jnp.broadcast_to(rb[:, -1:], (h, far)), rb[:, ::-1],
                           jnp.broadcast_to(rb[:, :1], (h, near))], axis=1)
    return ext.reshape(h, 1, CH_EXT)


def _build_bias_table(ext_ref, tab_ref):
    shape = (CH_TQ, CH_EXT)
    i = lax.broadcasted_iota(jnp.int32, shape, 0)
    j = lax.broadcasted_iota(jnp.int32, shape, 1)
    dchunk = j // CHUNK - i // CHUNK
    valid = (dchunk >= 0) & (dchunk <= LOOKBACK) & (j < CH_WIN)
    for h in range(H_CH):
        rolled = pltpu.roll(jnp.broadcast_to(ext_ref[h], shape), CH_EXT - (CH_TQ - 1), 1,
                            stride=1, stride_axis=0)
        tab = jnp.where(valid, rolled, NEG_INF)
        for o in range(CH_TAB_BLKS):
            tab_ref[h, o] = tab[:, o * LANES:(o + 1) * LANES]


def _ch_kernel(ext_ref, q_ref, k_ref, v_ref, o_ref, tab_ref):
    @pl.when((pl.program_id(0) == 0) & (pl.program_id(1) == 0))
    def _():
        _build_bias_table(ext_ref, tab_ref)

    npair = W_CH // LANES
    first_head = lax.broadcasted_iota(jnp.int32, (CH_TQ, LANES), 1) < HEAD_DIM
    units = [(sub, hp) for sub in range(CH_SUB) for hp in range(npair)]
    rows = [slice(sub * CH_TQ, (sub + 1) * CH_TQ) for sub in range(CH_SUB)]
    cols = [slice(hp * LANES, (hp + 1) * LANES) for hp in range(npair)]
    starts, shifts = [], []
    for sub in range(CH_SUB):
        nominal = (pl.program_id(1) * CH_SUB + sub) * CH_TQ - LOOKBACK * CHUNK
        start = pl.multiple_of(jnp.maximum(nominal, 0), LANES)
        starts.append(start)
        shifts.append((start - nominal) // LANES)

    def scores(u, _):
        sub, hp = units[u]
        return lax.dot_general(_stack_heads(q_ref[0, rows[sub], cols[hp]]),
                               k_ref[0, pl.ds(starts[sub], CH_WIN), cols[hp]],
                               _NT, preferred_element_type=F32)

    def softmax(u, z):
        sub, hp = units[u]
        bias = jnp.concatenate(
            [jnp.concatenate([tab_ref[2 * hp + h, shifts[sub] + o] for o in range(CH_NBLK)],
                             axis=-1) for h in range(2)], axis=0)
        z = z + bias
        m = jnp.max(z, axis=-1, keepdims=True)
        p = jnp.exp(z - m)
        return jnp.sum(p, axis=-1, keepdims=True), p.astype(BF16)

    def values(u, st):
        sub, hp = units[u]
        return st[0], jnp.dot(st[1], v_ref[0, pl.ds(starts[sub], CH_WIN), cols[hp]],
                              preferred_element_type=F32)

    def store(u, st):
        sub, hp = units[u]
        out = st[1] / st[0]
        o_ref[0, rows[sub], cols[hp]] = jnp.where(first_head, out[:CH_TQ], out[CH_TQ:])

    state = [None] * len(units)
    for stage in (scores, softmax, values, store):
        state = [stage(u, st) for u, st in enumerate(state)]


def _ch_attention(qkv, ext, b, s):
    base = 3 * W_SB // W_CH
    tq = CH_SUB * CH_TQ
    return pl.pallas_call(
        _ch_kernel,
        grid=(b, s // tq),
        in_specs=[_resident(ext.shape),
                  pl.BlockSpec((1, tq, W_CH), lambda bi, qi: (bi, qi, base)),
                  pl.BlockSpec((1, s, W_CH), lambda bi, qi: (bi, 0, base + 1)),
                  pl.BlockSpec((1, s, W_CH), lambda bi, qi: (bi, 0, base + 2))],
        out_specs=pl.BlockSpec((1, tq, W_CH), lambda bi, qi: (bi, qi, 0)),
        out_shape=jax.ShapeDtypeStruct((b, s, W_CH), F32),
        scratch_shapes=[pltpu.VMEM((H_CH, CH_TAB_BLKS, CH_TQ, LANES), F32)],
        compiler_params=pltpu.CompilerParams(
            dimension_semantics=("arbitrary", "arbitrary"),
            vmem_limit_bytes=ATTN_VMEM_BYTES),
        name="band_attn",
    )(ext, qkv, qkv, qkv)


def _tail_kernel(h_ref, oa_ref, ob_ref, p_ref, gsb_ref, gch_ref, wout_ref, gmixpost_ref,
                 gpre_ref, gpost_ref, wg_ref, wu_ref, wd_ref, wple_ref, wgate_ref, gple_ref,
                 out_ref):
    mixed = jnp.concatenate([_rms(oa_ref[...], gsb_ref[...]).astype(BF16),
                             _rms(ob_ref[...], gch_ref[...]).astype(BF16)], axis=-1)
    y = jnp.dot(mixed, wout_ref[...], preferred_element_type=F32)
    proj = jnp.dot(p_ref[...].astype(BF16), wple_ref[...], preferred_element_type=F32)
    h = h_ref[...] + _rms(y, gmixpost_ref[...])
    u = _rms(h, gpre_ref[...]).astype(BF16)
    f = _swiglu(u, wg_ref, wu_ref, wd_ref)
    h = h + 0.5 * _rms(f, gpost_ref[...])
    gate = jnp.dot(h.astype(BF16), wgate_ref[...], preferred_element_type=F32)
    e = proj * jax.nn.sigmoid(gate)
    out_ref[...] = h + _rms(e, gple_ref[...])


def _tail(h, oa, ob, p2, gsb, gch, wout, gmixpost, gpre, gpost, wg, wu, wd, wple, wgate, gple):
    n, d = h.shape
    d_ff = wg.shape[1]
    row = lambda w: pl.BlockSpec((ROW_TILE, w), lambda i: (i, 0))
    return pl.pallas_call(
        _tail_kernel,
        grid=(n // ROW_TILE,),
        in_specs=[row(d), row(oa.shape[1]), row(ob.shape[1]), row(p2.shape[1]),
                  _resident((1, oa.shape[1])), _resident((1, ob.shape[1])),
                  _resident(wout.shape), _resident((1, d)),
                  _resident((1, d)), _resident((1, d)),
                  _resident((d, d_ff)), _resident((d, d_ff)), _resident((d_ff, d)),
                  _resident(wple.shape), _resident(wgate.shape), _resident((1, d))],
        out_specs=row(d),
        out_shape=jax.ShapeDtypeStruct((n, d), F32),
        compiler_params=pltpu.CompilerParams(
            dimension_semantics=("arbitrary",), vmem_limit_bytes=DENSE_VMEM_BYTES),
        name="outproj_ffn2_ple",
    )(h, oa, ob, p2, gsb, gch, wout, gmixpost, gpre, gpost, wg, wu, wd, wple, wgate, gple)


def kernel(x, p, g_ffn1_pre, g_ffn1_post, w_ffn1_gate, w_ffn1_up, w_ffn1_down, g_mix_pre, g_mix_post, w_in, g_out_sb, g_out_ch, rel_bias, w_out, g_ffn2_pre, g_ffn2_post, w_ffn2_gate, w_ffn2_up, w_ffn2_down, w_ple_proj, w_ple_gate, g_ple_post):
    b, s, d = x.shape
    depth = p.shape[0]
    n = b * s
    gain = lambda g: g.astype(F32).reshape(1, -1)
    wcast = lambda w: w.astype(BF16)
    h = x.reshape(n, d)
    for i in range(depth):
        later = [w_out[i], w_ffn2_gate[i], w_ffn2_up[i], w_ffn2_down[i],
                 w_ple_proj[i], w_ple_gate[i]]
        h1, qkv, later_bf16 = _ffn_qkv(
            h, gain(g_ffn1_pre[i]), gain(g_ffn1_post[i]),
            wcast(w_ffn1_gate[i]), wcast(w_ffn1_up[i]), wcast(w_ffn1_down[i]),
            gain(g_mix_pre[i]), wcast(w_in[i]), later)
        wout, wg2, wu2, wd2, wple, wgate = later_bf16
        qkv3 = qkv.reshape(b, s, qkv.shape[1])
        o_a = _sb_attention(qkv3, b, s).reshape(n, W_SB)
        o_b = _ch_attention(qkv3, _bias_by_offset(rel_bias[i]), b, s).reshape(n, W_CH)
        h = _tail(h1, o_a, o_b, p[i].reshape(n, -1),
                  gain(g_out_sb[i]), gain(g_out_ch[i]), wout, gain(g_mix_post[i]),
                  gain(g_ffn2_pre[i]), gain(g_ffn2_post[i]), wg2, wu2, wd2,
                  wple, wgate, gain(g_ple_post[i]))
    return h.reshape(b, s, d)
```

```python
import functools

import jax
import jax.numpy as jnp
from jax import lax
from jax.experimental import pallas as pl
from jax.experimental.pallas import tpu as pltpu

HEAD_DIM = 64
H_SB = 8
H_CH = 8
W_SB = H_SB * HEAD_DIM
W_CH = H_CH * HEAD_DIM
CHUNK = 64
LOOKBACK = 8
REL_CLIP = 128
EPS = 1e-6
NEG_INF = -1e30
SCALE = HEAD_DIM ** -0.5

LANES = 128
BF16_SUBLANES = 16
MXU_DIM = 256
ROW_TILE = 512
FF_SPLIT = 2
SB_TILE = 256
CH_TQ = 128
CH_WIN = LOOKBACK * CHUNK + CH_TQ
CH_NBLK = CH_WIN // LANES
CH_TAB_BLKS = CH_NBLK + LOOKBACK * CHUNK // LANES
CH_EXT = (CH_TAB_BLKS + 1) * LANES
CH_SUB = 2
EXP_ZERO_BELOW = -104.0
DENSE_VMEM_BYTES = 58 * 1024 * 1024
ATTN_VMEM_BYTES = 48 * 1024 * 1024

F32 = jnp.float32
BF16 = jnp.bfloat16


def _rms(x, g):
    ms = jnp.mean(x * x, axis=-1, keepdims=True)
    return x * lax.rsqrt(ms + EPS) * g


def _swiglu(u, wg_ref, wu_ref, wd_ref):
    d_ff = wg_ref.shape[1]
    tiles = d_ff // MXU_DIM
    edges = [((c * tiles + FF_SPLIT - 1) // FF_SPLIT) * MXU_DIM for c in range(FF_SPLIT)] + [d_ff]
    f = None
    for lo, hi in zip(edges[:-1], edges[1:]):
        gate = jnp.dot(u, wg_ref[:, lo:hi], preferred_element_type=F32)
        up = jnp.dot(u, wu_ref[:, lo:hi], preferred_element_type=F32)
        act = (gate * jax.nn.sigmoid(gate) * up).astype(BF16)
        part = jnp.dot(act, wd_ref[lo:hi, :], preferred_element_type=F32)
        f = part if f is None else f + part
    return f


def _ffn_qkv_kernel(n_cast, x_ref, gpre_ref, gpost_ref, wg_ref, wu_ref, wd_ref,
                    gmix_ref, win_ref, *rest):
    cast_in, (h_ref, qkv_ref), cast_out = rest[:n_cast], rest[n_cast:n_cast + 2], rest[n_cast + 2:]
    for src, dst in zip(cast_in, cast_out):
        dst[...] = src[...].astype(BF16)
    x = x_ref[...]
    u = _rms(x, gpre_ref[...]).astype(BF16)
    f = _swiglu(u, wg_ref, wu_ref, wd_ref)
    h = x + 0.5 * _rms(f, gpost_ref[...])
    h_ref[...] = h
    u2 = _rms(h, gmix_ref[...]).astype(BF16)
    qkv_ref[...] = jnp.dot(u2, win_ref[...], preferred_element_type=F32).astype(BF16)


def _resident(shape):
    nd = len(shape)
    return pl.BlockSpec(shape, lambda *_: (0,) * nd, pipeline_mode=pl.Buffered(1))


def _cast_block_spec(shape, n_steps):
    rows, cols = shape
    blk = next(r for r in range(BF16_SUBLANES, rows + 1, BF16_SUBLANES)
               if rows % r == 0 and r * n_steps >= rows)
    last = rows // blk - 1
    return pl.BlockSpec((blk, cols), lambda i: (jnp.minimum(i, last), 0))


def _ffn_qkv(x2, gpre, gpost, wg, wu, wd, gmix, win, later_weights):
    n, d = x2.shape
    d_ff = wg.shape[1]
    d_qkv = win.shape[1]
    n_steps = n // ROW_TILE
    row = lambda w: pl.BlockSpec((ROW_TILE, w), lambda i: (i, 0))
    cast_specs = [_cast_block_spec(w.shape, n_steps) for w in later_weights]
    outs = pl.pallas_call(
        functools.partial(_ffn_qkv_kernel, len(later_weights)),
        grid=(n_steps,),
        in_specs=[row(d), _resident((1, d)), _resident((1, d)),
                  _resident((d, d_ff)), _resident((d, d_ff)), _resident((d_ff, d)),
                  _resident((1, d)), _resident((d, d_qkv))] + cast_specs,
        out_specs=[row(d), row(d_qkv)] + cast_specs,
        out_shape=[jax.ShapeDtypeStruct((n, d), F32),
                   jax.ShapeDtypeStruct((n, d_qkv), BF16)]
                  + [jax.ShapeDtypeStruct(w.shape, BF16) for w in later_weights],
        compiler_params=pltpu.CompilerParams(
            dimension_semantics=("arbitrary",), vmem_limit_bytes=DENSE_VMEM_BYTES),
        name="ffn1_qkv",
    )(x2, gpre, gpost, wg, wu, wd, gmix, win, *later_weights)
    return outs[0], outs[1], outs[2:]


def _stack_heads(q2):
    lane = lax.broadcasted_iota(jnp.int32, q2.shape, 1)
    zero = jnp.zeros_like(q2)
    qs = q2 * jnp.asarray(SCALE, q2.dtype)
    return jnp.concatenate([jnp.where(lane < HEAD_DIM, qs, zero),
                            jnp.where(lane >= HEAD_DIM, qs, zero)], axis=0)


_NT = (((1,), (1,)), ((), ()))


class _Chain:
    def __init__(self, n_units, stages, skewed):
        self.state = [None] * n_units
        if skewed:
            order = [(s - k, k) for s in range(n_units + len(stages) - 1)
                     for k in reversed(range(len(stages))) if 0 <= s - k < n_units]
        else:
            order = [(u, k) for k in range(len(stages)) for u in range(n_units)]
        self.steps = [functools.partial(self._apply, stages[k], u) for u, k in order]

    def _apply(self, stage, u):
        self.state[u] = stage(u, self.state[u])


def _emit_merged(*step_lists):
    tagged = sorted(((i + 0.5) / len(steps), n, i)
                    for n, steps in enumerate(step_lists) for i in range(len(steps)))
    for _, n, i in tagged:
        step_lists[n][i]()


def _stickbreak_tile(q_ref, k_ref, v_ref, o_ref, carry_ref, qi, other_steps):
    t = SB_TILE
    npair = W_SB // LANES
    pair_cols = [slice(hp * LANES, (hp + 1) * LANES) for hp in range(npair)]

    row = lax.broadcasted_iota(jnp.int32, (2 * t, t), 0)
    col = lax.broadcasted_iota(jnp.int32, (2 * t, t), 1)
    neg_after = jnp.where(lax.broadcasted_iota(jnp.int32, (t, t), 0)
                          > lax.broadcasted_iota(jnp.int32, (t, t), 1), -1.0, 0.0).astype(BF16)
    before = col < (row & (t - 1))
    first_head = lax.broadcasted_iota(jnp.int32, (t, LANES), 1) < HEAD_DIM
    q_stacks = [_stack_heads(q_ref[0, :, c]) for c in pair_cols]

    def sweep(kb, diag, gate=None):
        start = pl.multiple_of(kb * t, t)

        def scores(hp, _):
            return lax.dot_general(q_stacks[hp], k_ref[0, pl.ds(start, t), pair_cols[hp]], _NT,
                                   preferred_element_type=F32)

        def split(hp, z):
            sp = jnp.maximum(z, 0.0) + jnp.log(1.0 + jnp.exp(-jnp.abs(z)))
            if diag:
                sp = jnp.where(before, sp, 0.0)
            return z - sp, sp[:, 0:1], sp.astype(BF16)

        def cumsum(hp, st):
            return st[0], st[1], jnp.dot(st[2], neg_after, preferred_element_type=F32)

        def weights(hp, st):
            log_sig, sp_first, c = st
            p = jnp.exp(log_sig + c)
            if diag:
                p = jnp.where(before, p, 0.0)
            return c[:, 0:1] - sp_first, p.astype(BF16)

        def values(hp, st):
            return st[0], jnp.dot(st[1], v_ref[0, pl.ds(start, t), pair_cols[hp]],
                                  preferred_element_type=F32)

        def update(hp, st):
            tot, pv = st
            if diag:
                new = tot
                o_ref[0, :, pair_cols[hp]] = jnp.where(first_head, pv[:t], pv[t:])
            else:
                old = carry_ref[hp]
                scale = jnp.exp(old) if gate is None else jnp.exp(old) * gate
                pv, new = scale * pv, old + tot
                o_ref[0, :, pair_cols[hp]] += jnp.where(first_head, pv[:t], pv[t:])
            carry_ref[hp] = new
            return new

        return _Chain(npair, [scores, split, cumsum, weights, values, update], skewed=True)

    def worst_carry(chain):
        worst = chain.state[0]
        for c in chain.state[1:]:
            worst = jnp.maximum(worst, c)
        return jnp.max(worst)

    diagonal = sweep(qi, True)
    previous = sweep(jnp.maximum(qi - 1, 0), False, gate=(qi > 0).astype(F32))
    _emit_merged(diagonal.steps + previous.steps, other_steps)

    def cond(st):
        return jnp.logical_and(st[0] < qi, st[1] >= EXP_ZERO_BELOW)

    def body(st):
        chain = sweep(qi - 1 - st[0], False)
        _emit_merged(chain.steps)
        return st[0] + 1, worst_carry(chain)

    lax.while_loop(cond, body, (jnp.int32(1), worst_carry(previous)))


def _bias_by_offset(rel_bias):
    h = rel_bias.shape[0]
    rb = rel_bias.astype(F32)
    far = LOOKBACK * CHUNK + CH_TQ - 1 - REL_CLIP
    near = CH_EXT - far - (2 * REL_CLIP + 1)
    ext = jnp.concatenate([jnp.broadcast_to(rb[:, -1:], (h, far)), rb[:, ::-1],
                           jnp.broadcast_to(rb[:, :1], (h, near))], axis=1)
    return ext.reshape(h, 1, CH_EXT)


def _build_bias_table(ext_ref, tab_ref):
    shape = (CH_TQ, CH_EXT)
    i = lax.broadcasted_iota(jnp.int32, shape, 0)
    j = lax.broadcasted_iota(jnp.int32, shape, 1)
    dchunk = j // CHUNK - i // CHUNK
    valid = (dchunk >= 0) & (dchunk <= LOOKBACK) & (j < CH_WIN)
    for h in range(H_CH):
        rolled = pltpu.roll(jnp.broadcast_to(ext_ref[h], shape), CH_EXT - (CH_TQ - 1), 1,
                            stride=1, stride_axis=0)
        tab = jnp.where(valid, rolled, NEG_INF)
        for o in range(CH_TAB_BLKS):
            tab_ref[h, o] = tab[:, o * LANES:(o + 1) * LANES]


def _band_tiles(q_ref, k_ref, v_ref, o_ref, tab_ref, qi):
    npair = W_CH // LANES
    first_head = lax.broadcasted_iota(jnp.int32, (CH_TQ, LANES), 1) < HEAD_DIM
    units = [(sub, hp) for sub in range(CH_SUB) for hp in range(npair)]
    rows = [slice(sub * CH_TQ, (sub + 1) * CH_TQ) for sub in range(CH_SUB)]
    cols = [slice(hp * LANES, (hp + 1) * LANES) for hp in range(npair)]
    starts, shifts = [], []
    for sub in range(CH_SUB):
        nominal = (qi * CH_SUB + sub) * CH_TQ - LOOKBACK * CHUNK
        start = pl.multiple_of(jnp.maximum(nominal, 0), LANES)
        starts.append(start)
        shifts.append((start - nominal) // LANES)

    def scores(u, _):
        sub, hp = units[u]
        return lax.dot_general(_stack_heads(q_ref[0, rows[sub], cols[hp]]),
                               k_ref[0, pl.ds(starts[sub], CH_WIN), cols[hp]],
                               _NT, preferred_element_type=F32)

    def softmax(u, z):
        sub, hp = units[u]
        bias = jnp.concatenate(
            [jnp.concatenate([tab_ref[2 * hp + h, shifts[sub] + o] for o in range(CH_NBLK)],
                             axis=-1) for h in range(2)], axis=0)
        z = z + bias
        m = jnp.max(z, axis=-1, keepdims=True)
        p = jnp.exp(z - m)
        return jnp.sum(p, axis=-1, keepdims=True), p.astype(BF16)

    def values(u, st):
        sub, hp = units[u]
        return st[0], jnp.dot(st[1], v_ref[0, pl.ds(starts[sub], CH_WIN), cols[hp]],
                              preferred_element_type=F32)

    def store(u, st):
        sub, hp = units[u]
        out = st[1] / st[0]
        o_ref[0, rows[sub], cols[hp]] = jnp.where(first_head, out[:CH_TQ], out[CH_TQ:])

    return _Chain(len(units), [scores, softmax, values, store], skewed=False).steps


def _attn_kernel(ext_ref, qa_ref, ka_ref, va_ref, qb_ref, kb_ref, vb_ref, oa_ref, ob_ref,
                 carry_ref, tab_ref):
    @pl.when((pl.program_id(0) == 0) & (pl.program_id(1) == 0))
    def _():
        _build_bias_table(ext_ref, tab_ref)

    qi = pl.program_id(1)
    _stickbreak_tile(qa_ref, ka_ref, va_ref, oa_ref, carry_ref, qi,
                     _band_tiles(qb_ref, kb_ref, vb_ref, ob_ref, tab_ref, qi))


def _attention(qkv, ext, b, s):
    assert W_SB == W_CH and SB_TILE == CH_SUB * CH_TQ
    t, w = SB_TILE, W_SB
    tile = lambda c: pl.BlockSpec((1, t, w), lambda bi, qi: (bi, qi, c))
    whole = lambda c: pl.BlockSpec((1, s, w), lambda bi, qi: (bi, 0, c))
    return pl.pallas_call(
        _attn_kernel,
        grid=(b, s // t),
        in_specs=[_resident(ext.shape), tile(0), whole(1), whole(2), tile(3), whole(4), whole(5)],
        out_specs=[tile(0), tile(0)],
        out_shape=[jax.ShapeDtypeStruct((b, s, w), F32)] * 2,
        scratch_shapes=[pltpu.VMEM((w // LANES, 2 * t, 1), F32),
                        pltpu.VMEM((H_CH, CH_TAB_BLKS, CH_TQ, LANES), F32)],
        compiler_params=pltpu.CompilerParams(
            dimension_semantics=("arbitrary", "arbitrary"),
            vmem_limit_bytes=ATTN_VMEM_BYTES),
        name="mixers",
    )(ext, *([qkv] * 6))


def _tail_kernel(h_ref, oa_ref, ob_ref, p_ref, gsb_ref, gch_ref, wout_ref, gmixpost_ref,
                 gpre_ref, gpost_ref, wg_ref, wu_ref, wd_ref, wple_ref, wgate_ref, gple_ref,
                 out_ref):
    mixed = jnp.concatenate([_rms(oa_ref[...], gsb_ref[...]).astype(BF16),
                             _rms(ob_ref[...], gch_ref[...]).astype(BF16)], axis=-1)
    y = jnp.dot(mixed, wout_ref[...], preferred_element_type=F32)
    proj = jnp.dot(p_ref[...].astype(BF16), wple_ref[...], preferred_element_type=F32)
    h = h_ref[...] + _rms(y, gmixpost_ref[...])
    u = _rms(h, gpre_ref[...]).astype(BF16)
    f = _swiglu(u, wg_ref, wu_ref, wd_ref)
    h = h + 0.5 * _rms(f, gpost_ref[...])
    gate = jnp.dot(h.astype(BF16), wgate_ref[...], preferred_element_type=F32)
    e = proj * jax.nn.sigmoid(gate)
    out_ref[...] = h + _rms(e, gple_ref[...])


def _tail(h, oa, ob, p2, gsb, gch, wout, gmixpost, gpre, gpost, wg, wu, wd, wple, wgate, gple):
    n, d = h.shape
    d_ff = wg.shape[1]
    row = lambda w: pl.BlockSpec((ROW_TILE, w), lambda i: (i, 0))
    return pl.pallas_call(
        _tail_kernel,
        grid=(n // ROW_TILE,),
        in_specs=[row(d), row(oa.shape[1]), row(ob.shape[1]), row(p2.shape[1]),
                  _resident((1, oa.shape[1])), _resident((1, ob.shape[1])),
                  _resident(wout.shape), _resident((1, d)),
                  _resident((1, d)), _resident((1, d)),
                  _resident((d, d_ff)), _resident((d, d_ff)), _resident((d_ff, d)),
                  _resident(wple.shape), _resident(wgate.shape), _resident((1, d))],
        out_specs=row(d),
        out_shape=jax.ShapeDtypeStruct((n, d), F32),
        compiler_params=pltpu.CompilerParams(
            dimension_semantics=("arbitrary",), vmem_limit_bytes=DENSE_VMEM_BYTES),
        name="outproj_ffn2_ple",
    )(h, oa, ob, p2, gsb, gch, wout, gmixpost, gpre, gpost, wg, wu, wd, wple, wgate, gple)


def kernel(x, p, g_ffn1_pre, g_ffn1_post, w_ffn1_gate, w_ffn1_up, w_ffn1_down, g_mix_pre, g_mix_post, w_in, g_out_sb, g_out_ch, rel_bias, w_out, g_ffn2_pre, g_ffn2_post, w_ffn2_gate, w_ffn2_up, w_ffn2_down, w_ple_proj, w_ple_gate, g_ple_post):
    b, s, d = x.shape
    depth = p.shape[0]
    n = b * s
    gain = lambda g: g.astype(F32).reshape(1, -1)
    wcast = lambda w: w.astype(BF16)
    h = x.reshape(n, d)
    for i in range(depth):
        later = [w_out[i], w_ffn2_gate[i], w_ffn2_up[i], w_ffn2_down[i],
                 w_ple_proj[i], w_ple_gate[i]]
        h1, qkv, later_bf16 = _ffn_qkv(
            h, gain(g_ffn1_pre[i]), gain(g_ffn1_post[i]),
            wcast(w_ffn1_gate[i]), wcast(w_ffn1_up[i]), wcast(w_ffn1_down[i]),
            gain(g_mix_pre[i]), wcast(w_in[i]), later)
        wout, wg2, wu2, wd2, wple, wgate = later_bf16
        qkv3 = qkv.reshape(b, s, qkv.shape[1])
        o_a, o_b = _attention(qkv3, _bias_by_offset(rel_bias[i]), b, s)
        h = _tail(h1, o_a.reshape(n, W_SB), o_b.reshape(n, W_CH), p[i].reshape(n, -1),
                  gain(g_out_sb[i]), gain(g_out_ch[i]), wout, gain(g_mix_post[i]),
                  gain(g_ffn2_pre[i]), gain(g_ffn2_post[i]), wg2, wu2, wd2,
                  wple, wgate, gain(g_ple_post[i]))
    return h.reshape(b, s, d)
```

```python
import functools

import jax
import jax.numpy as jnp
from jax import lax
from jax.experimental import pallas as pl
from jax.experimental.pallas import tpu as pltpu

HEAD_DIM = 64
H_SB = 8
H_CH = 8
W_SB = H_SB * HEAD_DIM
W_CH = H_CH * HEAD_DIM
CHUNK = 64
LOOKBACK = 8
REL_CLIP = 128
EPS = 1e-6
NEG_INF = -1e30
SCALE = HEAD_DIM ** -0.5

LANES = 128
BF16_SUBLANES = 16
MXU_DIM = 256
ROW_TILE = 512
ROW_SPLIT = 2
FF_SPLIT = 2
SB_TILE = 256
CH_TQ = 128
CH_WIN = LOOKBACK * CHUNK + CH_TQ
CH_NBLK = CH_WIN // LANES
CH_TAB_BLKS = CH_NBLK + LOOKBACK * CHUNK // LANES
CH_EXT = (CH_TAB_BLKS + 1) * LANES
CH_SUB = 2
EXP_ZERO_BELOW = -104.0
DENSE_VMEM_BYTES = 58 * 1024 * 1024
ATTN_VMEM_BYTES = 48 * 1024 * 1024

F32 = jnp.float32
BF16 = jnp.bfloat16


def _rms(x, g):
    ms = jnp.mean(x * x, axis=-1, keepdims=True)
    return x * lax.rsqrt(ms + EPS) * g


def _swiglu(u, wg_ref, wu_ref, wd_ref):
    d_ff = wg_ref.shape[1]
    tiles = d_ff // MXU_DIM
    edges = [((c * tiles + FF_SPLIT - 1) // FF_SPLIT) * MXU_DIM for c in range(FF_SPLIT)] + [d_ff]
    f = None
    for lo, hi in zip(edges[:-1], edges[1:]):
        gate = jnp.dot(u, wg_ref[:, lo:hi], preferred_element_type=F32)
        up = jnp.dot(u, wu_ref[:, lo:hi], preferred_element_type=F32)
        act = (gate * jax.nn.sigmoid(gate) * up).astype(BF16)
        part = jnp.dot(act, wd_ref[lo:hi, :], preferred_element_type=F32)
        f = part if f is None else f + part
    return f


def _ffn_qkv_kernel(n_cast, x_ref, gpre_ref, gpost_ref, wg_ref, wu_ref, wd_ref,
                    gmix_ref, win_ref, *rest):
    cast_in, (h_ref, qkv_ref), cast_out = rest[:n_cast], rest[n_cast:n_cast + 2], rest[n_cast + 2:]
    for src, dst in zip(cast_in, cast_out):
        dst[...] = src[...].astype(BF16)
    rows = _sub_tiles()

    def prenorm(r, _):
        x = x_ref[rows[r], :]
        return x, _rms(x, gpre_ref[...]).astype(BF16)

    def ffn(r, st):
        return st[0], _swiglu(st[1], wg_ref, wu_ref, wd_ref)

    def residual(r, st):
        h = st[0] + 0.5 * _rms(st[1], gpost_ref[...])
        h_ref[rows[r], :] = h
        return _rms(h, gmix_ref[...]).astype(BF16)

    def project(r, u2):
        qkv_ref[rows[r], :] = jnp.dot(u2, win_ref[...], preferred_element_type=F32).astype(BF16)

    _emit_merged(_Chain(ROW_SPLIT, [prenorm, ffn, residual, project], skewed=True).steps)


def _sub_tiles():
    sub = ROW_TILE // ROW_SPLIT
    return [slice(r * sub, (r + 1) * sub) for r in range(ROW_SPLIT)]


def _resident(shape):
    nd = len(shape)
    return pl.BlockSpec(shape, lambda *_: (0,) * nd, pipeline_mode=pl.Buffered(1))


def _cast_block_spec(shape, n_steps):
    rows, cols = shape
    blk = next(r for r in range(BF16_SUBLANES, rows + 1, BF16_SUBLANES)
               if rows % r == 0 and r * n_steps >= rows)
    last = rows // blk - 1
    return pl.BlockSpec((blk, cols), lambda i: (jnp.minimum(i, last), 0))


def _ffn_qkv(x2, gpre, gpost, wg, wu, wd, gmix, win, later_weights):
    n, d = x2.shape
    d_ff = wg.shape[1]
    d_qkv = win.shape[1]
    n_steps = n // ROW_TILE
    row = lambda w: pl.BlockSpec((ROW_TILE, w), lambda i: (i, 0))
    cast_specs = [_cast_block_spec(w.shape, n_steps) for w in later_weights]
    outs = pl.pallas_call(
        functools.partial(_ffn_qkv_kernel, len(later_weights)),
        grid=(n_steps,),
        in_specs=[row(d), _resident((1, d)), _resident((1, d)),
                  _resident((d, d_ff)), _resident((d, d_ff)), _resident((d_ff, d)),
                  _resident((1, d)), _resident((d, d_qkv))] + cast_specs,
        out_specs=[row(d), row(d_qkv)] + cast_specs,
        out_shape=[jax.ShapeDtypeStruct((n, d), F32),
                   jax.ShapeDtypeStruct((n, d_qkv), BF16)]
                  + [jax.ShapeDtypeStruct(w.shape, BF16) for w in later_weights],
        compiler_params=pltpu.CompilerParams(
            dimension_semantics=("arbitrary",), vmem_limit_bytes=DENSE_VMEM_BYTES),
        name="ffn1_qkv",
    )(x2, gpre, gpost, wg, wu, wd, gmix, win, *later_weights)
    return outs[0], outs[1], outs[2:]


def _stack_heads(q2):
    lane = lax.broadcasted_iota(jnp.int32, q2.shape, 1)
    zero = jnp.zeros_like(q2)
    qs = q2 * jnp.asarray(SCALE, q2.dtype)
    return jnp.concatenate([jnp.where(lane < HEAD_DIM, qs, zero),
                            jnp.where(lane >= HEAD_DIM, qs, zero)], axis=0)


_NT = (((1,), (1,)), ((), ()))


class _Chain:
    def __init__(self, n_units, stages, skewed):
        self.state = [None] * n_units
        if skewed:
            order = [(s - k, k) for s in range(n_units + len(stages) - 1)
                     for k in reversed(range(len(stages))) if 0 <= s - k < n_units]
        else:
            order = [(u, k) for k in range(len(stages)) for u in range(n_units)]
        self.steps = [functools.partial(self._apply, stages[k], u) for u, k in order]

    def _apply(self, stage, u):
        self.state[u] = stage(u, self.state[u])


def _emit_merged(*step_lists):
    tagged = sorted(((i + 0.5) / len(steps), n, i)
                    for n, steps in enumerate(step_lists) for i in range(len(steps)))
    for _, n, i in tagged:
        step_lists[n][i]()


def _stickbreak_tile(q_ref, k_ref, v_ref, o_ref, carry_ref, qi, other_steps):
    t = SB_TILE
    npair = W_SB // LANES
    pair_cols = [slice(hp * LANES, (hp + 1) * LANES) for hp in range(npair)]

    row = lax.broadcasted_iota(jnp.int32, (2 * t, t), 0)
    col = lax.broadcasted_iota(jnp.int32, (2 * t, t), 1)
    neg_after = jnp.where(lax.broadcasted_iota(jnp.int32, (t, t), 0)
                          > lax.broadcasted_iota(jnp.int32, (t, t), 1), -1.0, 0.0).astype(BF16)
    before = col < (row & (t - 1))
    first_head = lax.broadcasted_iota(jnp.int32, (t, LANES), 1) < HEAD_DIM
    q_stacks = [_stack_heads(q_ref[0, :, c]) for c in pair_cols]

    def sweep(kb, diag, gate=None):
        start = pl.multiple_of(kb * t, t)

        def scores(hp, _):
            return lax.dot_general(q_stacks[hp], k_ref[0, pl.ds(start, t), pair_cols[hp]], _NT,
                                   preferred_element_type=F32)

        def split(hp, z):
            sp = jnp.maximum(z, 0.0) + jnp.log(1.0 + jnp.exp(-jnp.abs(z)))
            if diag:
                sp = jnp.where(before, sp, 0.0)
            return z - sp, sp[:, 0:1], sp.astype(BF16)

        def cumsum(hp, st):
            return st[0], st[1], jnp.dot(st[2], neg_after, preferred_element_type=F32)

        def weights(hp, st):
            log_sig, sp_first, c = st
            p = jnp.exp(log_sig + c)
            if diag:
                p = jnp.where(before, p, 0.0)
            return c[:, 0:1] - sp_first, p.astype(BF16)

        def values(hp, st):
            return st[0], jnp.dot(st[1], v_ref[0, pl.ds(start, t), pair_cols[hp]],
                                  preferred_element_type=F32)

        def update(hp, st):
            tot, pv = st
            if diag:
                new = tot
                o_ref[0, :, pair_cols[hp]] = jnp.where(first_head, pv[:t], pv[t:])
            else:
                old = carry_ref[hp]
                scale = jnp.exp(old) if gate is None else jnp.exp(old) * gate
                pv, new = scale * pv, old + tot
                o_ref[0, :, pair_cols[hp]] += jnp.where(first_head, pv[:t], pv[t:])
            carry_ref[hp] = new
            return new

        return _Chain(npair, [scores, split, cumsum, weights, values, update], skewed=True)

    def worst_carry(chain):
        worst = chain.state[0]
        for c in chain.state[1:]:
            worst = jnp.maximum(worst, c)
        return jnp.max(worst)

    diagonal = sweep(qi, True)
    previous = sweep(jnp.maximum(qi - 1, 0), False, gate=(qi > 0).astype(F32))
    _emit_merged(diagonal.steps + previous.steps, other_steps)

    def cond(st):
        return jnp.logical_and(st[0] < qi, st[1] >= EXP_ZERO_BELOW)

    def body(st):
        chain = sweep(qi - 1 - st[0], False)
        _emit_merged(chain.steps)
        return st[0] + 1, worst_carry(chain)

    lax.while_loop(cond, body, (jnp.int32(1), worst_carry(previous)))


def _bias_by_offset(rel_bias):
    h = rel_bias.shape[0]
    rb = rel_bias.astype(F32)
    far = LOOKBACK * CHUNK + CH_TQ - 1 - REL_CLIP
    near = CH_EXT - far - (2 * REL_CLIP + 1)
    ext = jnp.concatenate([jnp.broadcast_to(rb[:, -1:], (h, far)), rb[:, ::-1],
                           jnp.broadcast_to(rb[:, :1], (h, near))], axis=1)
    return ext.reshape(h, 1, CH_EXT)


def _build_bias_table(ext_ref, tab_ref):
    shape = (CH_TQ, CH_EXT)
    i = lax.broadcasted_iota(jnp.int32, shape, 0)
    j = lax.broadcasted_iota(jnp.int32, shape, 1)
    dchunk = j // CHUNK - i // CHUNK
    valid = (dchunk >= 0) & (dchunk <= LOOKBACK) & (j < CH_WIN)
    for h in range(H_CH):
        rolled = pltpu.roll(jnp.broadcast_to(ext_ref[h], shape), CH_EXT - (CH_TQ - 1), 1,
                            stride=1, stride_axis=0)
        tab = jnp.where(valid, rolled, NEG_INF)
        for o in range(CH_TAB_BLKS):
            tab_ref[h, o] = tab[:, o * LANES:(o + 1) * LANES]


def _band_tiles(q_ref, k_ref, v_ref, o_ref, tab_ref, qi):
    npair = W_CH // LANES
    first_head = lax.broadcasted_iota(jnp.int32, (CH_TQ, LANES), 1) < HEAD_DIM
    units = [(sub, hp) for sub in range(CH_SUB) for hp in range(npair)]
    rows = [slice(sub * CH_TQ, (sub + 1) * CH_TQ) for sub in range(CH_SUB)]
    cols = [slice(hp * LANES, (hp + 1) * LANES) for hp in range(npair)]
    starts, shifts = [], []
    for sub in range(CH_SUB):
        nominal = (qi * CH_SUB + sub) * CH_TQ - LOOKBACK * CHUNK
        start = pl.multiple_of(jnp.maximum(nominal, 0), LANES)
        starts.append(start)
        shifts.append((start - nominal) // LANES)

    def scores(u, _):
        sub, hp = units[u]
        return lax.dot_general(_stack_heads(q_ref[0, rows[sub], cols[hp]]),
                               k_ref[0, pl.ds(starts[sub], CH_WIN), cols[hp]],
                               _NT, preferred_element_type=F32)

    def softmax(u, z):
        sub, hp = units[u]
        bias = jnp.concatenate(
            [jnp.concatenate([tab_ref[2 * hp + h, shifts[sub] + o] for o in range(CH_NBLK)],
                             axis=-1) for h in range(2)], axis=0)
        z = z + bias
        m = jnp.max(z, axis=-1, keepdims=True)
        p = jnp.exp(z - m)
        return jnp.sum(p, axis=-1, keepdims=True), p.astype(BF16)

    def values(u, st):
        sub, hp = units[u]
        return st[0], jnp.dot(st[1], v_ref[0, pl.ds(starts[sub], CH_WIN), cols[hp]],
                              preferred_element_type=F32)

    def store(u, st):
        sub, hp = units[u]
        out = st[1] / st[0]
        o_ref[0, rows[sub], cols[hp]] = jnp.where(first_head, out[:CH_TQ], out[CH_TQ:])

    return _Chain(len(units), [scores, softmax, values, store], skewed=False).steps


def _attn_kernel(ext_ref, qa_ref, ka_ref, va_ref, qb_ref, kb_ref, vb_ref, oa_ref, ob_ref,
                 carry_ref, tab_ref):
    @pl.when((pl.program_id(0) == 0) & (pl.program_id(1) == 0))
    def _():
        _build_bias_table(ext_ref, tab_ref)

    qi = pl.program_id(1)
    _stickbreak_tile(qa_ref, ka_ref, va_ref, oa_ref, carry_ref, qi,
                     _band_tiles(qb_ref, kb_ref, vb_ref, ob_ref, tab_ref, qi))


def _attention(qkv, ext, b, s):
    assert W_SB == W_CH and SB_TILE == CH_SUB * CH_TQ
    t, w = SB_TILE, W_SB
    tile = lambda c: pl.BlockSpec((1, t, w), lambda bi, qi: (bi, qi, c))
    whole = lambda c: pl.BlockSpec((1, s, w), lambda bi, qi: (bi, 0, c))
    return pl.pallas_call(
        _attn_kernel,
        grid=(b, s // t),
        in_specs=[_resident(ext.shape), tile(0), whole(1), whole(2), tile(3), whole(4), whole(5)],
        out_specs=[tile(0), tile(0)],
        out_shape=[jax.ShapeDtypeStruct((b, s, w), F32)] * 2,
        scratch_shapes=[pltpu.VMEM((w // LANES, 2 * t, 1), F32),
                        pltpu.VMEM((H_CH, CH_TAB_BLKS, CH_TQ, LANES), F32)],
        compiler_params=pltpu.CompilerParams(
            dimension_semantics=("arbitrary", "arbitrary"),
            vmem_limit_bytes=ATTN_VMEM_BYTES),
        name="mixers",
    )(ext, *([qkv] * 6))


def _tail_kernel(h_ref, oa_ref, ob_ref, p_ref, gsb_ref, gch_ref, wout_ref, gmixpost_ref,
                 gpre_ref, gpost_ref, wg_ref, wu_ref, wd_ref, wple_ref, wgate_ref, gple_ref,
                 out_ref):
    rows = _sub_tiles()

    def mix_norm(r, _):
        return jnp.concatenate([_rms(oa_ref[rows[r], :], gsb_ref[...]).astype(BF16),
                                _rms(ob_ref[rows[r], :], gch_ref[...]).astype(BF16)], axis=-1)

    def out_proj(r, mixed):
        y = jnp.dot(mixed, wout_ref[...], preferred_element_type=F32)
        proj = jnp.dot(p_ref[rows[r], :].astype(BF16), wple_ref[...], preferred_element_type=F32)
        return y, proj

    def mix_residual(r, st):
        h = h_ref[rows[r], :] + _rms(st[0], gmixpost_ref[...])
        return h, st[1], _rms(h, gpre_ref[...]).astype(BF16)

    def ffn(r, st):
        return st[0], st[1], _swiglu(st[2], wg_ref, wu_ref, wd_ref)

    def ffn_residual(r, st):
        h = st[0] + 0.5 * _rms(st[2], gpost_ref[...])
        return h, st[1], h.astype(BF16)

    def ple_gate(r, st):
        return st[0], st[1], jnp.dot(st[2], wgate_ref[...], preferred_element_type=F32)

    def ple_residual(r, st):
        h, proj, gate = st
        e = proj * jax.nn.sigmoid(gate)
        out_ref[rows[r], :] = h + _rms(e, gple_ref[...])

    _emit_merged(_Chain(ROW_SPLIT, [mix_norm, out_proj, mix_residual, ffn, ffn_residual,
                                    ple_gate, ple_residual], skewed=True).steps)


def _tail(h, oa, ob, p2, gsb, gch, wout, gmixpost, gpre, gpost, wg, wu, wd, wple, wgate, gple):
    n, d = h.shape
    d_ff = wg.shape[1]
    row = lambda w: pl.BlockSpec((ROW_TILE, w), lambda i: (i, 0))
    return pl.pallas_call(
        _tail_kernel,
        grid=(n // ROW_TILE,),
        in_specs=[row(d), row(oa.shape[1]), row(ob.shape[1]), row(p2.shape[1]),
                  _resident((1, oa.shape[1])), _resident((1, ob.shape[1])),
                  _resident(wout.shape), _resident((1, d)),
                  _resident((1, d)), _resident((1, d)),
                  _resident((d, d_ff)), _resident((d, d_ff)), _resident((d_ff, d)),
                  _resident(wple.shape), _resident(wgate.shape), _resident((1, d))],
        out_specs=row(d),
        out_shape=jax.ShapeDtypeStruct((n, d), F32),
        compiler_params=pltpu.CompilerParams(
            dimension_semantics=("arbitrary",), vmem_limit_bytes=DENSE_VMEM_BYTES),
        name="outproj_ffn2_ple",
    )(h, oa, ob, p2, gsb, gch, wout, gmixpost, gpre, gpost, wg, wu, wd, wple, wgate, gple)


def kernel(x, p, g_ffn1_pre, g_ffn1_post, w_ffn1_gate, w_ffn1_up, w_ffn1_down, g_mix_pre, g_mix_post, w_in, g_out_sb, g_out_ch, rel_bias, w_out, g_ffn2_pre, g_ffn2_post, w_ffn2_gate, w_ffn2_up, w_ffn2_down, w_ple_proj, w_ple_gate, g_ple_post):
    b, s, d = x.shape
    depth = p.shape[0]
    n = b * s
    gain = lambda g: g.astype(F32).reshape(1, -1)
    wcast = lambda w: w.astype(BF16)
    h = x.reshape(n, d)
    for i in range(depth):
        later = [w_out[i], w_ffn2_gate[i], w_ffn2_up[i], w_ffn2_down[i],
                 w_ple_proj[i], w_ple_gate[i]]
        h1, qkv, later_bf16 = _ffn_qkv(
            h, gain(g_ffn1_pre[i]), gain(g_ffn1_post[i]),
            wcast(w_ffn1_gate[i]), wcast(w_ffn1_up[i]), wcast(w_ffn1_down[i]),
            gain(g_mix_pre[i]), wcast(w_in[i]), later)
        wout, wg2, wu2, wd2, wple, wgate = later_bf16
        qkv3 = qkv.reshape(b, s, qkv.shape[1])
        o_a, o_b = _attention(qkv3, _bias_by_offset(rel_bias[i]), b, s)
        h = _tail(h1, o_a.reshape(n, W_SB), o_b.reshape(n, W_CH), p[i].reshape(n, -1),
                  gain(g_out_sb[i]), gain(g_out_ch[i]), wout, gain(g_mix_post[i]),
                  gain(g_ffn2_pre[i]), gain(g_ffn2_post[i]), wg2, wu2, wd2,
                  wple, wgate, gain(g_ple_post[i]))
    return h.reshape(b, s, d)
```

```python
import functools

import jax
import jax.numpy as jnp
from jax import lax
from jax.experimental import pallas as pl
from jax.experimental.pallas import tpu as pltpu

HEAD_DIM = 64
H_SB = 8
H_CH = 8
W_SB = H_SB * HEAD_DIM
W_CH = H_CH * HEAD_DIM
CHUNK = 64
LOOKBACK = 8
REL_CLIP = 128
EPS = 1e-6
NEG_INF = -1e30
SCALE = HEAD_DIM ** -0.5

LANES = 128
BF16_SUBLANES = 16
MXU_DIM = 256
ROW_TILE = 512
ROW_SPLIT = 2
WEIGHT_STEPS = 8
FF_SPLIT = 2
SB_TILE = 256
CH_TQ = 128
CH_WIN = LOOKBACK * CHUNK + CH_TQ
CH_NBLK = CH_WIN // LANES
CH_TAB_BLKS = CH_NBLK + LOOKBACK * CHUNK // LANES
CH_EXT = (CH_TAB_BLKS + 1) * LANES
CH_SUB = 2
EXP_ZERO_BELOW = -104.0
DENSE_VMEM_BYTES = 58 * 1024 * 1024
ATTN_VMEM_BYTES = 48 * 1024 * 1024

F32 = jnp.float32
BF16 = jnp.bfloat16


def _rms(x, g):
    ms = jnp.mean(x * x, axis=-1, keepdims=True)
    return x * lax.rsqrt(ms + EPS) * g


def _swiglu(u, wg_ref, wu_ref, wd_ref):
    d_ff = wg_ref.shape[1]
    tiles = d_ff // MXU_DIM
    edges = [((c * tiles + FF_SPLIT - 1) // FF_SPLIT) * MXU_DIM for c in range(FF_SPLIT)] + [d_ff]
    f = None
    for lo, hi in zip(edges[:-1], edges[1:]):
        gate = jnp.dot(u, wg_ref[:, lo:hi], preferred_element_type=F32)
        up = jnp.dot(u, wu_ref[:, lo:hi], preferred_element_type=F32)
        act = (gate * jax.nn.sigmoid(gate) * up).astype(BF16)
        part = jnp.dot(act, wd_ref[lo:hi, :], preferred_element_type=F32)
        f = part if f is None else f + part
    return f


def _ffn_qkv_kernel(n_cast, x_ref, gpre_ref, gpost_ref, wg_blk, wu_blk, wd_blk,
                    gmix_ref, win_blk, *rest):
    cast_in, (h_ref, qkv_ref) = rest[:n_cast], rest[n_cast:n_cast + 2]
    cast_out = rest[n_cast + 2:2 * n_cast + 2]
    wg_ref, wu_ref, wd_ref, win_ref = rest[2 * n_cast + 2:]
    step = pl.program_id(0)

    @pl.when(step < WEIGHT_STEPS)
    def _():
        for blk, dst in ((wg_blk, wg_ref), (wu_blk, wu_ref), (wd_blk, wd_ref), (win_blk, win_ref)):
            nr = blk.shape[0]
            dst[pl.ds(pl.multiple_of(step * nr, nr), nr), :] = blk[...].astype(BF16)

    @pl.when(step >= WEIGHT_STEPS)
    def _():
        _ffn_qkv_tile(x_ref, gpre_ref, gpost_ref, wg_ref, wu_ref, wd_ref, gmix_ref, win_ref,
                      cast_in, h_ref, qkv_ref, cast_out)


def _ffn_qkv_tile(x_ref, gpre_ref, gpost_ref, wg_ref, wu_ref, wd_ref, gmix_ref, win_ref,
                  cast_in, h_ref, qkv_ref, cast_out):
    for src, dst in zip(cast_in, cast_out):
        dst[...] = src[...].astype(BF16)
    rows = _sub_tiles()

    def prenorm(r, _):
        x = x_ref[rows[r], :]
        return x, _rms(x, gpre_ref[...]).astype(BF16)

    def ffn(r, st):
        return st[0], _swiglu(st[1], wg_ref, wu_ref, wd_ref)

    def residual(r, st):
        h = st[0] + 0.5 * _rms(st[1], gpost_ref[...])
        h_ref[rows[r], :] = h
        return _rms(h, gmix_ref[...]).astype(BF16)

    def project(r, u2):
        qkv_ref[rows[r], :] = jnp.dot(u2, win_ref[...], preferred_element_type=F32).astype(BF16)

    _emit_merged(_Chain(ROW_SPLIT, [prenorm, ffn, residual, project], skewed=True).steps)


def _sub_tiles():
    sub = ROW_TILE // ROW_SPLIT
    return [slice(r * sub, (r + 1) * sub) for r in range(ROW_SPLIT)]


def _resident(shape):
    nd = len(shape)
    return pl.BlockSpec(shape, lambda *_: (0,) * nd, pipeline_mode=pl.Buffered(1))


def _cast_block_spec(shape, n_steps, first_step):
    rows, cols = shape
    blk = next(r for r in range(BF16_SUBLANES, rows + 1, BF16_SUBLANES)
               if rows % r == 0 and r * n_steps >= rows)
    last = rows // blk - 1
    return pl.BlockSpec((blk, cols), lambda i: (jnp.clip(i - first_step, 0, last), 0))


def _ffn_qkv(x2, gpre, gpost, wg, wu, wd, gmix, win, later_weights):
    n, d = x2.shape
    n_steps = n // ROW_TILE
    own = [wg, wu, wd, win]
    assert all(w.shape[0] % (WEIGHT_STEPS * BF16_SUBLANES) == 0 for w in own)
    row = lambda w: pl.BlockSpec((ROW_TILE, w), lambda i: (jnp.maximum(i - WEIGHT_STEPS, 0), 0))
    own_specs = [pl.BlockSpec((w.shape[0] // WEIGHT_STEPS, w.shape[1]),
                              lambda i: (jnp.minimum(i, WEIGHT_STEPS - 1), 0)) for w in own]
    cast_specs = [_cast_block_spec(w.shape, n_steps, WEIGHT_STEPS) for w in later_weights]
    outs = pl.pallas_call(
        functools.partial(_ffn_qkv_kernel, len(later_weights)),
        grid=(WEIGHT_STEPS + n_steps,),
        in_specs=[row(d), _resident((1, d)), _resident((1, d)), *own_specs[:3],
                  _resident((1, d)), own_specs[3]] + cast_specs,
        out_specs=[row(d), row(win.shape[1])] + cast_specs,
        out_shape=[jax.ShapeDtypeStruct((n, d), F32),
                   jax.ShapeDtypeStruct((n, win.shape[1]), BF16)]
                  + [jax.ShapeDtypeStruct(w.shape, BF16) for w in later_weights],
        scratch_shapes=[pltpu.VMEM(w.shape, BF16) for w in own],
        compiler_params=pltpu.CompilerParams(
            dimension_semantics=("arbitrary",), vmem_limit_bytes=DENSE_VMEM_BYTES),
        name="ffn1_qkv",
    )(x2, gpre, gpost, wg, wu, wd, gmix, win, *later_weights)
    return outs[0], outs[1], outs[2:]


def _stack_heads(q2):
    lane = lax.broadcasted_iota(jnp.int32, q2.shape, 1)
    zero = jnp.zeros_like(q2)
    qs = q2 * jnp.asarray(SCALE, q2.dtype)
    return jnp.concatenate([jnp.where(lane < HEAD_DIM, qs, zero),
                            jnp.where(lane >= HEAD_DIM, qs, zero)], axis=0)


_NT = (((1,), (1,)), ((), ()))


class _Chain:
    def __init__(self, n_units, stages, skewed):
        self.state = [None] * n_units
        if skewed:
            order = [(s - k, k) for s in range(n_units + len(stages) - 1)
                     for k in reversed(range(len(stages))) if 0 <= s - k < n_units]
        else:
            order = [(u, k) for k in range(len(stages)) for u in range(n_units)]
        self.steps = [functools.partial(self._apply, stages[k], u) for u, k in order]

    def _apply(self, stage, u):
        self.state[u] = stage(u, self.state[u])


def _emit_merged(*step_lists):
    tagged = sorted(((i + 0.5) / len(steps), n, i)
                    for n, steps in enumerate(step_lists) for i in range(len(steps)))
    for _, n, i in tagged:
        step_lists[n][i]()


def _stickbreak_tile(q_ref, k_ref, v_ref, o_ref, carry_ref, qi, other_steps):
    t = SB_TILE
    npair = W_SB // LANES
    pair_cols = [slice(hp * LANES, (hp + 1) * LANES) for hp in range(npair)]

    row = lax.broadcasted_iota(jnp.int32, (2 * t, t), 0)
    col = lax.broadcasted_iota(jnp.int32, (2 * t, t), 1)
    neg_after = jnp.where(lax.broadcasted_iota(jnp.int32, (t, t), 0)
                          > lax.broadcasted_iota(jnp.int32, (t, t), 1), -1.0, 0.0).astype(BF16)
    before = col < (row & (t - 1))
    first_head = lax.broadcasted_iota(jnp.int32, (t, LANES), 1) < HEAD_DIM
    q_stacks = [_stack_heads(q_ref[0, :, c]) for c in pair_cols]

    def sweep(kb, diag, gate=None):
        start = pl.multiple_of(kb * t, t)

        def scores(hp, _):
            return lax.dot_general(q_stacks[hp], k_ref[0, pl.ds(start, t), pair_cols[hp]], _NT,
                                   preferred_element_type=F32)

        def split(hp, z):
            sp = jnp.maximum(z, 0.0) + jnp.log(1.0 + jnp.exp(-jnp.abs(z)))
            if diag:
                sp = jnp.where(before, sp, 0.0)
            return z - sp, sp[:, 0:1], sp.astype(BF16)

        def cumsum(hp, st):
            return st[0], st[1], jnp.dot(st[2], neg_after, preferred_element_type=F32)

        def weights(hp, st):
            log_sig, sp_first, c = st
            p = jnp.exp(log_sig + c)
            if diag:
                p = jnp.where(before, p, 0.0)
            return c[:, 0:1] - sp_first, p.astype(BF16)

        def values(hp, st):
            return st[0], jnp.dot(st[1], v_ref[0, pl.ds(start, t), pair_cols[hp]],
                                  preferred_element_type=F32)

        def update(hp, st):
            tot, pv = st
            if diag:
                new = tot
                o_ref[0, :, pair_cols[hp]] = jnp.where(first_head, pv[:t], pv[t:])
            else:
                old = carry_ref[hp]
                scale = jnp.exp(old) if gate is None else jnp.exp(old) * gate
                pv, new = scale * pv, old + tot
                o_ref[0, :, pair_cols[hp]] += jnp.where(first_head, pv[:t], pv[t:])
            carry_ref[hp] = new
            return new

        return _Chain(npair, [scores, split, cumsum, weights, values, update], skewed=True)

    def worst_carry(chain):
        worst = chain.state[0]
        for c in chain.state[1:]:
            worst = jnp.maximum(worst, c)
        return jnp.max(worst)

    diagonal = sweep(qi, True)
    previous = sweep(jnp.maximum(qi - 1, 0), False, gate=(qi > 0).astype(F32))
    _emit_merged(diagonal.steps + previous.steps, other_steps)

    def cond(st):
        return jnp.logical_and(st[0] < qi, st[1] >= EXP_ZERO_BELOW)

    def body(st):
        chain = sweep(qi - 1 - st[0], False)
        _emit_merged(chain.steps)
        return st[0] + 1, worst_carry(chain)

    lax.while_loop(cond, body, (jnp.int32(1), worst_carry(previous)))


def _bias_by_offset(rel_bias):
    h = rel_bias.shape[0]
    rb = rel_bias.astype(F32)
    far = LOOKBACK * CHUNK + CH_TQ - 1 - REL_CLIP
    near = CH_EXT - far - (2 * REL_CLIP + 1)
    ext = jnp.concatenate([jnp.broadcast_to(rb[:, -1:], (h, far)), rb[:, ::-1],
                           jnp.broadcast_to(rb[:, :1], (h, near))], axis=1)
    return ext.reshape(h, 1, CH_EXT)


def _build_bias_table(ext_ref, tab_ref):
    shape = (CH_TQ, CH_EXT)
    i = lax.broadcasted_iota(jnp.int32, shape, 0)
    j = lax.broadcasted_iota(jnp.int32, shape, 1)
    dchunk = j // CHUNK - i // CHUNK
    valid = (dchunk >= 0) & (dchunk <= LOOKBACK) & (j < CH_WIN)
    for h in range(H_CH):
        rolled = pltpu.roll(jnp.broadcast_to(ext_ref[h], shape), CH_EXT - (CH_TQ - 1), 1,
                            stride=1, stride_axis=0)
        tab = jnp.where(valid, rolled, NEG_INF)
        for o in range(CH_TAB_BLKS):
            tab_ref[h, o] = tab[:, o * LANES:(o + 1) * LANES]


def _band_tiles(q_ref, k_ref, v_ref, o_ref, tab_ref, qi):
    npair = W_CH // LANES
    first_head = lax.broadcasted_iota(jnp.int32, (CH_TQ, LANES), 1) < HEAD_DIM
    units = [(sub, hp) for sub in range(CH_SUB) for hp in range(npair)]
    rows = [slice(sub * CH_TQ, (sub + 1) * CH_TQ) for sub in range(CH_SUB)]
    cols = [slice(hp * LANES, (hp + 1) * LANES) for hp in range(npair)]
    starts, shifts = [], []
    for sub in range(CH_SUB):
        nominal = (qi * CH_SUB + sub) * CH_TQ - LOOKBACK * CHUNK
        start = pl.multiple_of(jnp.maximum(nominal, 0), LANES)
        starts.append(start)
        shifts.append((start - nominal) // LANES)

    def scores(u, _):
        sub, hp = units[u]
        return lax.dot_general(_stack_heads(q_ref[0, rows[sub], cols[hp]]),
                               k_ref[0, pl.ds(starts[sub], CH_WIN), cols[hp]],
                               _NT, preferred_element_type=F32)

    def softmax(u, z):
        sub, hp = units[u]
        bias = jnp.concatenate(
            [jnp.concatenate([tab_ref[2 * hp + h, shifts[sub] + o] for o in range(CH_NBLK)],
                             axis=-1) for h in range(2)], axis=0)
        z = z + bias
        m = jnp.max(z, axis=-1, keepdims=True)
        p = jnp.exp(z - m)
        return jnp.sum(p, axis=-1, keepdims=True), p.astype(BF16)

    def values(u, st):
        sub, hp = units[u]
        return st[0], jnp.dot(st[1], v_ref[0, pl.ds(starts[sub], CH_WIN), cols[hp]],
                              preferred_element_type=F32)

    def store(u, st):
        sub, hp = units[u]
        out = st[1] / st[0]
        o_ref[0, rows[sub], cols[hp]] = jnp.where(first_head, out[:CH_TQ], out[CH_TQ:])

    return _Chain(len(units), [scores, softmax, values, store], skewed=False).steps


def _attn_kernel(ext_ref, qa_ref, ka_ref, va_ref, qb_ref, kb_ref, vb_ref, oa_ref, ob_ref,
                 carry_ref, tab_ref):
    @pl.when((pl.program_id(0) == 0) & (pl.program_id(1) == 0))
    def _():
        _build_bias_table(ext_ref, tab_ref)

    qi = pl.program_id(1)
    _stickbreak_tile(qa_ref, ka_ref, va_ref, oa_ref, carry_ref, qi,
                     _band_tiles(qb_ref, kb_ref, vb_ref, ob_ref, tab_ref, qi))


def _attention(qkv, ext, b, s):
    assert W_SB == W_CH and SB_TILE == CH_SUB * CH_TQ
    t, w = SB_TILE, W_SB
    tile = lambda c: pl.BlockSpec((1, t, w), lambda bi, qi: (bi, qi, c))
    whole = lambda c: pl.BlockSpec((1, s, w), lambda bi, qi: (bi, 0, c))
    return pl.pallas_call(
        _attn_kernel,
        grid=(b, s // t),
        in_specs=[_resident(ext.shape), tile(0), whole(1), whole(2), tile(3), whole(4), whole(5)],
        out_specs=[tile(0), tile(0)],
        out_shape=[jax.ShapeDtypeStruct((b, s, w), F32)] * 2,
        scratch_shapes=[pltpu.VMEM((w // LANES, 2 * t, 1), F32),
                        pltpu.VMEM((H_CH, CH_TAB_BLKS, CH_TQ, LANES), F32)],
        compiler_params=pltpu.CompilerParams(
            dimension_semantics=("arbitrary", "arbitrary"),
            vmem_limit_bytes=ATTN_VMEM_BYTES),
        name="mixers",
    )(ext, *([qkv] * 6))


def _tail_kernel(h_ref, oa_ref, ob_ref, p_ref, gsb_ref, gch_ref, wout_ref, gmixpost_ref,
                 gpre_ref, gpost_ref, wg_ref, wu_ref, wd_ref, wple_ref, wgate_ref, gple_ref,
                 out_ref):
    rows = _sub_tiles()

    def mix_norm(r, _):
        return jnp.concatenate([_rms(oa_ref[rows[r], :], gsb_ref[...]).astype(BF16),
                                _rms(ob_ref[rows[r], :], gch_ref[...]).astype(BF16)], axis=-1)

    def out_proj(r, mixed):
        y = jnp.dot(mixed, wout_ref[...], preferred_element_type=F32)
        proj = jnp.dot(p_ref[rows[r], :].astype(BF16), wple_ref[...], preferred_element_type=F32)
        return y, proj

    def mix_residual(r, st):
        h = h_ref[rows[r], :] + _rms(st[0], gmixpost_ref[...])
        return h, st[1], _rms(h, gpre_ref[...]).astype(BF16)

    def ffn(r, st):
        return st[0], st[1], _swiglu(st[2], wg_ref, wu_ref, wd_ref)

    def ffn_residual(r, st):
        h = st[0] + 0.5 * _rms(st[2], gpost_ref[...])
        return h, st[1], h.astype(BF16)

    def ple_gate(r, st):
        return st[0], st[1], jnp.dot(st[2], wgate_ref[...], preferred_element_type=F32)

    def ple_residual(r, st):
        h, proj, gate = st
        e = proj * jax.nn.sigmoid(gate)
        out_ref[rows[r], :] = h + _rms(e, gple_ref[...])

    _emit_merged(_Chain(ROW_SPLIT, [mix_norm, out_proj, mix_residual, ffn, ffn_residual,
                                    ple_gate, ple_residual], skewed=True).steps)


def _tail(h, oa, ob, p2, gsb, gch, wout, gmixpost, gpre, gpost, wg, wu, wd, wple, wgate, gple):
    n, d = h.shape
    d_ff = wg.shape[1]
    row = lambda w: pl.BlockSpec((ROW_TILE, w), lambda i: (i, 0))
    return pl.pallas_call(
        _tail_kernel,
        grid=(n // ROW_TILE,),
        in_specs=[row(d), row(oa.shape[1]), row(ob.shape[1]), row(p2.shape[1]),
                  _resident((1, oa.shape[1])), _resident((1, ob.shape[1])),
                  _resident(wout.shape), _resident((1, d)),
                  _resident((1, d)), _resident((1, d)),
                  _resident((d, d_ff)), _resident((d, d_ff)), _resident((d_ff, d)),
                  _resident(wple.shape), _resident(wgate.shape), _resident((1, d))],
        out_specs=row(d),
        out_shape=jax.ShapeDtypeStruct((n, d), F32),
        compiler_params=pltpu.CompilerParams(
            dimension_semantics=("arbitrary",), vmem_limit_bytes=DENSE_VMEM_BYTES),
        name="outproj_ffn2_ple",
    )(h, oa, ob, p2, gsb, gch, wout, gmixpost, gpre, gpost, wg, wu, wd, wple, wgate, gple)


def kernel(x, p, g_ffn1_pre, g_ffn1_post, w_ffn1_gate, w_ffn1_up, w_ffn1_down, g_mix_pre, g_mix_post, w_in, g_out_sb, g_out_ch, rel_bias, w_out, g_ffn2_pre, g_ffn2_post, w_ffn2_gate, w_ffn2_up, w_ffn2_down, w_ple_proj, w_ple_gate, g_ple_post):
    b, s, d = x.shape
    depth = p.shape[0]
    n = b * s
    gain = lambda g: g.astype(F32).reshape(1, -1)
    h = x.reshape(n, d)
    for i in range(depth):
        later = [w_out[i], w_ffn2_gate[i], w_ffn2_up[i], w_ffn2_down[i],
                 w_ple_proj[i], w_ple_gate[i]]
        h1, qkv, later_bf16 = _ffn_qkv(
            h, gain(g_ffn1_pre[i]), gain(g_ffn1_post[i]),
            w_ffn1_gate[i], w_ffn1_up[i], w_ffn1_down[i], gain(g_mix_pre[i]), w_in[i], later)
        wout, wg2, wu2, wd2, wple, wgate = later_bf16
        qkv3 = qkv.reshape(b, s, qkv.shape[1])
        o_a, o_b = _attention(qkv3, _bias_by_offset(rel_bias[i]), b, s)
        h = _tail(h1, o_a.reshape(n, W_SB), o_b.reshape(n, W_CH), p[i].reshape(n, -1),
                  gain(g_out_sb[i]), gain(g_out_ch[i]), wout, gain(g_mix_post[i]),
                  gain(g_ffn2_pre[i]), gain(g_ffn2_post[i]), wg2, wu2, wd2,
                  wple, wgate, gain(g_ple_post[i]))
    return h.reshape(b, s, d)
```

```python
import functools

import jax
import jax.numpy as jnp
from jax import lax
from jax.experimental import pallas as pl
from jax.experimental.pallas import tpu as pltpu

HEAD_DIM = 64
H_SB = 8
H_CH = 8
W_SB = H_SB * HEAD_DIM
W_CH = H_CH * HEAD_DIM
CHUNK = 64
LOOKBACK = 8
REL_CLIP = 128
EPS = 1e-6
NEG_INF = -1e30
SCALE = HEAD_DIM ** -0.5

LANES = 128
BF16_SUBLANES = 16
MXU_DIM = 256
ROW_TILE = 512
ROW_SPLIT = 2
WEIGHT_STEPS = 8
FF_SPLIT = 2
SB_TILE = 256
SB_NEAR_ROWS = 160
CH_TQ = 128
CH_WIN = LOOKBACK * CHUNK + CH_TQ
CH_NBLK = CH_WIN // LANES
CH_TAB_BLKS = CH_NBLK + LOOKBACK * CHUNK // LANES
CH_EXT = (CH_TAB_BLKS + 1) * LANES
CH_SUB = 2
EXP_ZERO_BELOW = -104.0
DENSE_VMEM_BYTES = 58 * 1024 * 1024
ATTN_VMEM_BYTES = 48 * 1024 * 1024

F32 = jnp.float32
BF16 = jnp.bfloat16


def _rms(x, g):
    ms = jnp.mean(x * x, axis=-1, keepdims=True)
    return x * lax.rsqrt(ms + EPS) * g


def _swiglu(u, wg_ref, wu_ref, wd_ref):
    d_ff = wg_ref.shape[1]
    tiles = d_ff // MXU_DIM
    edges = [((c * tiles + FF_SPLIT - 1) // FF_SPLIT) * MXU_DIM for c in range(FF_SPLIT)] + [d_ff]
    f = None
    for lo, hi in zip(edges[:-1], edges[1:]):
        gate = jnp.dot(u, wg_ref[:, lo:hi], preferred_element_type=F32)
        up = jnp.dot(u, wu_ref[:, lo:hi], preferred_element_type=F32)
        act = (gate * jax.nn.sigmoid(gate) * up).astype(BF16)
        part = jnp.dot(act, wd_ref[lo:hi, :], preferred_element_type=F32)
        f = part if f is None else f + part
    return f


def _ffn_qkv_kernel(n_cast, x_ref, gpre_ref, gpost_ref, wg_blk, wu_blk, wd_blk,
                    gmix_ref, win_blk, *rest):
    cast_in, (h_ref, qkv_ref) = rest[:n_cast], rest[n_cast:n_cast + 2]
    cast_out = rest[n_cast + 2:2 * n_cast + 2]
    wg_ref, wu_ref, wd_ref, win_ref = rest[2 * n_cast + 2:]
    step = pl.program_id(0)

    @pl.when(step < WEIGHT_STEPS)
    def _():
        for blk, dst in ((wg_blk, wg_ref), (wu_blk, wu_ref), (wd_blk, wd_ref), (win_blk, win_ref)):
            nr = blk.shape[0]
            dst[pl.ds(pl.multiple_of(step * nr, nr), nr), :] = blk[...].astype(BF16)

    @pl.when(step >= WEIGHT_STEPS)
    def _():
        _ffn_qkv_tile(x_ref, gpre_ref, gpost_ref, wg_ref, wu_ref, wd_ref, gmix_ref, win_ref,
                      cast_in, h_ref, qkv_ref, cast_out)


def _ffn_qkv_tile(x_ref, gpre_ref, gpost_ref, wg_ref, wu_ref, wd_ref, gmix_ref, win_ref,
                  cast_in, h_ref, qkv_ref, cast_out):
    for src, dst in zip(cast_in, cast_out):
        dst[...] = src[...].astype(BF16)
    rows = _sub_tiles()

    def prenorm(r, _):
        x = x_ref[rows[r], :]
        return x, _rms(x, gpre_ref[...]).astype(BF16)

    def ffn(r, st):
        return st[0], _swiglu(st[1], wg_ref, wu_ref, wd_ref)

    def residual(r, st):
        h = st[0] + 0.5 * _rms(st[1], gpost_ref[...])
        h_ref[rows[r], :] = h
        return _rms(h, gmix_ref[...]).astype(BF16)

    def project(r, u2):
        qkv_ref[rows[r], :] = jnp.dot(u2, win_ref[...], preferred_element_type=F32).astype(BF16)

    _emit_merged(_Chain(ROW_SPLIT, [prenorm, ffn, residual, project], skewed=True).steps)


def _sub_tiles():
    sub = ROW_TILE // ROW_SPLIT
    return [slice(r * sub, (r + 1) * sub) for r in range(ROW_SPLIT)]


def _resident(shape):
    nd = len(shape)
    return pl.BlockSpec(shape, lambda *_: (0,) * nd, pipeline_mode=pl.Buffered(1))


def _cast_block_spec(shape, n_steps, first_step):
    rows, cols = shape
    blk = next(r for r in range(BF16_SUBLANES, rows + 1, BF16_SUBLANES)
               if rows % r == 0 and r * n_steps >= rows)
    last = rows // blk - 1
    return pl.BlockSpec((blk, cols), lambda i: (jnp.clip(i - first_step, 0, last), 0))


def _ffn_qkv(x2, gpre, gpost, wg, wu, wd, gmix, win, later_weights):
    n, d = x2.shape
    n_steps = n // ROW_TILE
    own = [wg, wu, wd, win]
    assert all(w.shape[0] % (WEIGHT_STEPS * BF16_SUBLANES) == 0 for w in own)
    row = lambda w: pl.BlockSpec((ROW_TILE, w), lambda i: (jnp.maximum(i - WEIGHT_STEPS, 0), 0))
    own_specs = [pl.BlockSpec((w.shape[0] // WEIGHT_STEPS, w.shape[1]),
                              lambda i: (jnp.minimum(i, WEIGHT_STEPS - 1), 0)) for w in own]
    cast_specs = [_cast_block_spec(w.shape, n_steps, WEIGHT_STEPS) for w in later_weights]
    outs = pl.pallas_call(
        functools.partial(_ffn_qkv_kernel, len(later_weights)),
        grid=(WEIGHT_STEPS + n_steps,),
        in_specs=[row(d), _resident((1, d)), _resident((1, d)), *own_specs[:3],
                  _resident((1, d)), own_specs[3]] + cast_specs,
        out_specs=[row(d), row(win.shape[1])] + cast_specs,
        out_shape=[jax.ShapeDtypeStruct((n, d), F32),
                   jax.ShapeDtypeStruct((n, win.shape[1]), BF16)]
                  + [jax.ShapeDtypeStruct(w.shape, BF16) for w in later_weights],
        scratch_shapes=[pltpu.VMEM(w.shape, BF16) for w in own],
        compiler_params=pltpu.CompilerParams(
            dimension_semantics=("arbitrary",), vmem_limit_bytes=DENSE_VMEM_BYTES),
        name="ffn1_qkv",
    )(x2, gpre, gpost, wg, wu, wd, gmix, win, *later_weights)
    return outs[0], outs[1], outs[2:]


def _stack_heads(q2):
    lane = lax.broadcasted_iota(jnp.int32, q2.shape, 1)
    zero = jnp.zeros_like(q2)
    qs = q2 * jnp.asarray(SCALE, q2.dtype)
    return jnp.concatenate([jnp.where(lane < HEAD_DIM, qs, zero),
                            jnp.where(lane >= HEAD_DIM, qs, zero)], axis=0)


_NT = (((1,), (1,)), ((), ()))


class _Chain:
    def __init__(self, n_units, stages, skewed):
        self.state = [None] * n_units
        if skewed:
            order = [(s - k, k) for s in range(n_units + len(stages) - 1)
                     for k in reversed(range(len(stages))) if 0 <= s - k < n_units]
        else:
            order = [(u, k) for k in range(len(stages)) for u in range(n_units)]
        self.steps = [functools.partial(self._apply, stages[k], u) for u, k in order]

    def _apply(self, stage, u):
        self.state[u] = stage(u, self.state[u])


def _emit_merged(*step_lists):
    tagged = sorted(((i + 0.5) / len(steps), n, i)
                    for n, steps in enumerate(step_lists) for i in range(len(steps)))
    for _, n, i in tagged:
        step_lists[n][i]()


def _stickbreak_tile(q_ref, k_ref, v_ref, o_ref, carry_ref, qi, other_steps):
    t = SB_TILE
    npair = W_SB // LANES
    pair_cols = [slice(hp * LANES, (hp + 1) * LANES) for hp in range(npair)]

    row = lax.broadcasted_iota(jnp.int32, (2 * t, t), 0)
    col = lax.broadcasted_iota(jnp.int32, (2 * t, t), 1)
    neg_after = jnp.where(lax.broadcasted_iota(jnp.int32, (t, t), 0)
                          > lax.broadcasted_iota(jnp.int32, (t, t), 1), -1.0, 0.0).astype(BF16)
    before = col < (row & (t - 1))
    first_head = lax.broadcasted_iota(jnp.int32, (t, LANES), 1) < HEAD_DIM
    q_stacks = [_stack_heads(q_ref[0, :, c]) for c in pair_cols]

    def sweep(kb, diag, gate=None, rows=(0, t)):
        start = pl.multiple_of(kb * t, t)
        r0, nr = rows
        assert not (diag and nr != t)
        heads_rows = (slice(r0, r0 + nr), slice(t + r0, t + r0 + nr))

        def scores(hp, _):
            q = q_stacks[hp]
            if nr != t:
                q = jnp.concatenate([q[s] for s in heads_rows], axis=0)
            return lax.dot_general(q, k_ref[0, pl.ds(start, t), pair_cols[hp]], _NT,
                                   preferred_element_type=F32)

        def split(hp, z):
            sp = jnp.maximum(z, 0.0) + jnp.log(1.0 + jnp.exp(-jnp.abs(z)))
            if diag:
                sp = jnp.where(before, sp, 0.0)
            return z - sp, sp[:, 0:1], sp.astype(BF16)

        def cumsum(hp, st):
            return st[0], st[1], jnp.dot(st[2], neg_after, preferred_element_type=F32)

        def weights(hp, st):
            log_sig, sp_first, c = st
            p = jnp.exp(log_sig + c)
            if diag:
                p = jnp.where(before, p, 0.0)
            return c[:, 0:1] - sp_first, p.astype(BF16)

        def values(hp, st):
            return st[0], jnp.dot(st[1], v_ref[0, pl.ds(start, t), pair_cols[hp]],
                                  preferred_element_type=F32)

        def update(hp, st):
            tot, pv = st
            if diag:
                new = tot
                o_ref[0, :, pair_cols[hp]] = jnp.where(first_head, pv[:t], pv[t:])
            else:
                old = jnp.concatenate([carry_ref[hp, s] for s in heads_rows], axis=0)
                scale = jnp.exp(old) if gate is None else jnp.exp(old) * gate
                pv, new = scale * pv, old + tot
                first = lax.broadcasted_iota(jnp.int32, (nr, LANES), 1) < HEAD_DIM
                o_ref[0, heads_rows[0], pair_cols[hp]] += jnp.where(first, pv[:nr], pv[nr:])
            carry_ref[hp, heads_rows[0]] = new[:nr]
            carry_ref[hp, heads_rows[1]] = new[nr:]
            return new

        return _Chain(npair, [scores, split, cumsum, weights, values, update], skewed=True)

    def worst_carry(carries, rows=(0, t)):
        r0, nr = rows
        worst = None
        for c in carries:
            for part in (c[r0:r0 + nr], c[t + r0:t + r0 + nr]):
                worst = part if worst is None else jnp.maximum(worst, part)
        return jnp.max(worst)

    near, far = (0, SB_NEAR_ROWS), (SB_NEAR_ROWS, t - SB_NEAR_ROWS)
    diagonal = sweep(qi, True)
    previous = sweep(jnp.maximum(qi - 1, 0), False, gate=(qi > 0).astype(F32), rows=near)
    _emit_merged(diagonal.steps + previous.steps, other_steps)

    @pl.when(jnp.logical_and(qi > 0, worst_carry(diagonal.state, far) >= EXP_ZERO_BELOW))
    def _():
        _emit_merged(sweep(qi - 1, False, rows=far).steps)

    def cond(st):
        return jnp.logical_and(st[0] < qi, st[1] >= EXP_ZERO_BELOW)

    def body(st):
        chain = sweep(qi - 1 - st[0], False)
        _emit_merged(chain.steps)
        return st[0] + 1, worst_carry(chain.state)

    lax.while_loop(cond, body,
                   (jnp.int32(1), worst_carry([carry_ref[hp] for hp in range(npair)])))


def _bias_by_offset(rel_bias):
    h = rel_bias.shape[0]
    rb = rel_bias.astype(F32)
    far = LOOKBACK * CHUNK + CH_TQ - 1 - REL_CLIP
    near = CH_EXT - far - (2 * REL_CLIP + 1)
    ext = jnp.concatenate([jnp.broadcast_to(rb[:, -1:], (h, far)), rb[:, ::-1],
                           jnp.broadcast_to(rb[:, :1], (h, near))], axis=1)
    return ext.reshape(h, 1, CH_EXT)


def _build_bias_table(ext_ref, tab_ref):
    shape = (CH_TQ, CH_EXT)
    i = lax.broadcasted_iota(jnp.int32, shape, 0)
    j = lax.broadcasted_iota(jnp.int32, shape, 1)
    dchunk = j // CHUNK - i // CHUNK
    valid = (dchunk >= 0) & (dchunk <= LOOKBACK) & (j < CH_WIN)
    for h in range(H_CH):
        rolled = pltpu.roll(jnp.broadcast_to(ext_ref[h], shape), CH_EXT - (CH_TQ - 1), 1,
                            stride=1, stride_axis=0)
        tab = jnp.where(valid, rolled, NEG_INF)
        for o in range(CH_TAB_BLKS):
            tab_ref[h, o] = tab[:, o * LANES:(o + 1) * LANES]


def _band_tiles(q_ref, k_ref, v_ref, o_ref, tab_ref, qi):
    npair = W_CH // LANES
    first_head = lax.broadcasted_iota(jnp.int32, (CH_TQ, LANES), 1) < HEAD_DIM
    units = [(sub, hp) for sub in range(CH_SUB) for hp in range(npair)]
    rows = [slice(sub * CH_TQ, (sub + 1) * CH_TQ) for sub in range(CH_SUB)]
    cols = [slice(hp * LANES, (hp + 1) * LANES) for hp in range(npair)]
    starts, shifts = [], []
    for sub in range(CH_SUB):
        nominal = (qi * CH_SUB + sub) * CH_TQ - LOOKBACK * CHUNK
        start = pl.multiple_of(jnp.maximum(nominal, 0), LANES)
        starts.append(start)
        shifts.append((start - nominal) // LANES)

    def scores(u, _):
        sub, hp = units[u]
        return lax.dot_general(_stack_heads(q_ref[0, rows[sub], cols[hp]]),
                               k_ref[0, pl.ds(starts[sub], CH_WIN), cols[hp]],
                               _NT, preferred_element_type=F32)

    def softmax(u, z):
        sub, hp = units[u]
        bias = jnp.concatenate(
            [jnp.concatenate([tab_ref[2 * hp + h, shifts[sub] + o] for o in range(CH_NBLK)],
                             axis=-1) for h in range(2)], axis=0)
        z = z + bias
        m = jnp.max(z, axis=-1, keepdims=True)
        p = jnp.exp(z - m)
        return jnp.sum(p, axis=-1, keepdims=True), p.astype(BF16)

    def values(u, st):
        sub, hp = units[u]
        return st[0], jnp.dot(st[1], v_ref[0, pl.ds(starts[sub], CH_WIN), cols[hp]],
                              preferred_element_type=F32)

    def store(u, st):
        sub, hp = units[u]
        out = st[1] / st[0]
        o_ref[0, rows[sub], cols[hp]] = jnp.where(first_head, out[:CH_TQ], out[CH_TQ:])

    return _Chain(len(units), [scores, softmax, values, store], skewed=False).steps


def _attn_kernel(ext_ref, qa_ref, ka_ref, va_ref, qb_ref, kb_ref, vb_ref, oa_ref, ob_ref,
                 carry_ref, tab_ref):
    @pl.when((pl.program_id(0) == 0) & (pl.program_id(1) == 0))
    def _():
        _build_bias_table(ext_ref, tab_ref)

    qi = pl.program_id(1)
    _stickbreak_tile(qa_ref, ka_ref, va_ref, oa_ref, carry_ref, qi,
                     _band_tiles(qb_ref, kb_ref, vb_ref, ob_ref, tab_ref, qi))


def _attention(qkv, ext, b, s):
    assert W_SB == W_CH and SB_TILE == CH_SUB * CH_TQ
    t, w = SB_TILE, W_SB
    tile = lambda c: pl.BlockSpec((1, t, w), lambda bi, qi: (bi, qi, c))
    whole = lambda c: pl.BlockSpec((1, s, w), lambda bi, qi: (bi, 0, c))
    return pl.pallas_call(
        _attn_kernel,
        grid=(b, s // t),
        in_specs=[_resident(ext.shape), tile(0), whole(1), whole(2), tile(3), whole(4), whole(5)],
        out_specs=[tile(0), tile(0)],
        out_shape=[jax.ShapeDtypeStruct((b, s, w), F32)] * 2,
        scratch_shapes=[pltpu.VMEM((w // LANES, 2 * t, 1), F32),
                        pltpu.VMEM((H_CH, CH_TAB_BLKS, CH_TQ, LANES), F32)],
        compiler_params=pltpu.CompilerParams(
            dimension_semantics=("arbitrary", "arbitrary"),
            vmem_limit_bytes=ATTN_VMEM_BYTES),
        name="mixers",
    )(ext, *([qkv] * 6))


def _tail_kernel(h_ref, oa_ref, ob_ref, p_ref, gsb_ref, gch_ref, wout_ref, gmixpost_ref,
                 gpre_ref, gpost_ref, wg_ref, wu_ref, wd_ref, wple_ref, wgate_ref, gple_ref,
                 out_ref):
    rows = _sub_tiles()

    def mix_norm(r, _):
        return jnp.concatenate([_rms(oa_ref[rows[r], :], gsb_ref[...]).astype(BF16),
                                _rms(ob_ref[rows[r], :], gch_ref[...]).astype(BF16)], axis=-1)

    def out_proj(r, mixed):
        y = jnp.dot(mixed, wout_ref[...], preferred_element_type=F32)
        proj = jnp.dot(p_ref[rows[r], :].astype(BF16), wple_ref[...], preferred_element_type=F32)
        return y, proj

    def mix_residual(r, st):
        h = h_ref[rows[r], :] + _rms(st[0], gmixpost_ref[...])
        return h, st[1], _rms(h, gpre_ref[...]).astype(BF16)

    def ffn(r, st):
        return st[0], st[1], _swiglu(st[2], wg_ref, wu_ref, wd_ref)

    def ffn_residual(r, st):
        h = st[0] + 0.5 * _rms(st[2], gpost_ref[...])
        return h, st[1], h.astype(BF16)

    def ple_gate(r, st):
        return st[0], st[1], jnp.dot(st[2], wgate_ref[...], preferred_element_type=F32)

    def ple_residual(r, st):
        h, proj, gate = st
        e = proj * jax.nn.sigmoid(gate)
        out_ref[rows[r], :] = h + _rms(e, gple_ref[...])

    _emit_merged(_Chain(ROW_SPLIT, [mix_norm, out_proj, mix_residual, ffn, ffn_residual,
                                    ple_gate, ple_residual], skewed=True).steps)


def _tail(h, oa, ob, p2, gsb, gch, wout, gmixpost, gpre, gpost, wg, wu, wd, wple, wgate, gple):
    n, d = h.shape
    d_ff = wg.shape[1]
    row = lambda w: pl.BlockSpec((ROW_TILE, w), lambda i: (i, 0))
    return pl.pallas_call(
        _tail_kernel,
        grid=(n // ROW_TILE,),
        in_specs=[row(d), row(oa.shape[1]), row(ob.shape[1]), row(p2.shape[1]),
                  _resident((1, oa.shape[1])), _resident((1, ob.shape[1])),
                  _resident(wout.shape), _resident((1, d)),
                  _resident((1, d)), _resident((1, d)),
                  _resident((d, d_ff)), _resident((d, d_ff)), _resident((d_ff, d)),
                  _resident(wple.shape), _resident(wgate.shape), _resident((1, d))],
        out_specs=row(d),
        out_shape=jax.ShapeDtypeStruct((n, d), F32),
        compiler_params=pltpu.CompilerParams(
            dimension_semantics=("arbitrary",), vmem_limit_bytes=DENSE_VMEM_BYTES),
        name="outproj_ffn2_ple",
    )(h, oa, ob, p2, gsb, gch, wout, gmixpost, gpre, gpost, wg, wu, wd, wple, wgate, gple)


def kernel(x, p, g_ffn1_pre, g_ffn1_post, w_ffn1_gate, w_ffn1_up, w_ffn1_down, g_mix_pre, g_mix_post, w_in, g_out_sb, g_out_ch, rel_bias, w_out, g_ffn2_pre, g_ffn2_post, w_ffn2_gate, w_ffn2_up, w_ffn2_down, w_ple_proj, w_ple_gate, g_ple_post):
    b, s, d = x.shape
    depth = p.shape[0]
    n = b * s
    gain = lambda g: g.astype(F32).reshape(1, -1)
    h = x.reshape(n, d)
    for i in range(depth):
        later = [w_out[i], w_ffn2_gate[i], w_ffn2_up[i], w_ffn2_down[i],
                 w_ple_proj[i], w_ple_gate[i]]
        h1, qkv, later_bf16 = _ffn_qkv(
            h, gain(g_ffn1_pre[i]), gain(g_ffn1_post[i]),
            w_ffn1_gate[i], w_ffn1_up[i], w_ffn1_down[i], gain(g_mix_pre[i]), w_in[i], later)
        wout, wg2, wu2, wd2, wple, wgate = later_bf16
        qkv3 = qkv.reshape(b, s, qkv.shape[1])
        o_a, o_b = _attention(qkv3, _bias_by_offset(rel_bias[i]), b, s)
        h = _tail(h1, o_a.reshape(n, W_SB), o_b.reshape(n, W_CH), p[i].reshape(n, -1),
                  gain(g_out_sb[i]), gain(g_out_ch[i]), wout, gain(g_mix_post[i]),
                  gain(g_ffn2_pre[i]), gain(g_ffn2_post[i]), wg2, wu2, wd2,
                  wple, wgate, gain(g_ple_post[i]))
    return h.reshape(b, s, d)
```

```python
import functools

import jax
import jax.numpy as jnp
from jax import lax
from jax.experimental import pallas as pl
from jax.experimental.pallas import tpu as pltpu

HEAD_DIM = 64
H_SB = 8
H_CH = 8
W_SB = H_SB * HEAD_DIM
W_CH = H_CH * HEAD_DIM
CHUNK = 64
LOOKBACK = 8
REL_CLIP = 128
EPS = 1e-6
NEG_INF = -1e30
SCALE = HEAD_DIM ** -0.5

LANES = 128
BF16_SUBLANES = 16
MXU_DIM = 256
ROW_TILE = 512
ROW_SPLIT = 2
WEIGHT_STEPS = 8
FF_SPLIT = 2
SB_TILE = 256
SB_NEAR_ROWS = 160
CH_TQ = 128
CH_WIN = LOOKBACK * CHUNK + CH_TQ
CH_NBLK = CH_WIN // LANES
CH_TAB_BLKS = CH_NBLK + LOOKBACK * CHUNK // LANES
CH_EXT = (CH_TAB_BLKS + 1) * LANES
CH_SUB = 2
EXP_ZERO_BELOW = -104.0
DENSE_VMEM_BYTES = 58 * 1024 * 1024
ATTN_VMEM_BYTES = 48 * 1024 * 1024

F32 = jnp.float32
BF16 = jnp.bfloat16


def _rms(x, g):
    ms = jnp.mean(x * x, axis=-1, keepdims=True)
    return x * lax.rsqrt(ms + EPS) * g


def _swiglu(u, wg_ref, wu_ref, wd_ref):
    d_ff = wg_ref.shape[1]
    tiles = d_ff // MXU_DIM
    edges = [((c * tiles + FF_SPLIT - 1) // FF_SPLIT) * MXU_DIM for c in range(FF_SPLIT)] + [d_ff]
    f = None
    for lo, hi in zip(edges[:-1], edges[1:]):
        gate = jnp.dot(u, wg_ref[:, lo:hi], preferred_element_type=F32)
        up = jnp.dot(u, wu_ref[:, lo:hi], preferred_element_type=F32)
        act = (gate * jax.nn.sigmoid(gate) * up).astype(BF16)
        part = jnp.dot(act, wd_ref[lo:hi, :], preferred_element_type=F32)
        f = part if f is None else f + part
    return f


def _ffn_qkv_kernel(n_cast, x_ref, gpre_ref, gpost_ref, wg_blk, wu_blk, wd_blk,
                    gmix_ref, win_blk, *rest):
    cast_in, (h_ref, qkv_ref) = rest[:n_cast], rest[n_cast:n_cast + 2]
    cast_out = rest[n_cast + 2:2 * n_cast + 2]
    wg_ref, wu_ref, wd_ref, win_ref = rest[2 * n_cast + 2:]
    step = pl.program_id(0)

    @pl.when(step < WEIGHT_STEPS)
    def _():
        for blk, dst in ((wg_blk, wg_ref), (wu_blk, wu_ref), (wd_blk, wd_ref), (win_blk, win_ref)):
            nr = blk.shape[0]
            dst[pl.ds(pl.multiple_of(step * nr, nr), nr), :] = blk[...].astype(BF16)

    @pl.when(step >= WEIGHT_STEPS)
    def _():
        _ffn_qkv_tile(x_ref, gpre_ref, gpost_ref, wg_ref, wu_ref, wd_ref, gmix_ref, win_ref,
                      cast_in, h_ref, qkv_ref, cast_out)


def _ffn_qkv_tile(x_ref, gpre_ref, gpost_ref, wg_ref, wu_ref, wd_ref, gmix_ref, win_ref,
                  cast_in, h_ref, qkv_ref, cast_out):
    for src, dst in zip(cast_in, cast_out):
        dst[...] = src[...].astype(BF16)
    rows = _sub_tiles()

    def prenorm(r, _):
        x = x_ref[rows[r], :]
        return x, _rms(x, gpre_ref[...]).astype(BF16)

    def ffn(r, st):
        return st[0], _swiglu(st[1], wg_ref, wu_ref, wd_ref)

    def residual(r, st):
        h = st[0] + 0.5 * _rms(st[1], gpost_ref[...])
        h_ref[rows[r], :] = h
        return _rms(h, gmix_ref[...]).astype(BF16)

    def project(r, u2):
        qkv_ref[rows[r], :] = jnp.dot(u2, win_ref[...], preferred_element_type=F32).astype(BF16)

    _emit_merged(_Chain(ROW_SPLIT, [prenorm, ffn, residual, project], skewed=True).steps)


def _sub_tiles():
    sub = ROW_TILE // ROW_SPLIT
    return [slice(r * sub, (r + 1) * sub) for r in range(ROW_SPLIT)]


def _resident(shape):
    nd = len(shape)
    return pl.BlockSpec(shape, lambda *_: (0,) * nd, pipeline_mode=pl.Buffered(1))


def _cast_block_spec(shape, n_steps, first_step):
    rows, cols = shape
    blk = next(r for r in range(BF16_SUBLANES, rows + 1, BF16_SUBLANES)
               if rows % r == 0 and r * n_steps >= rows)
    last = rows // blk - 1
    return pl.BlockSpec((blk, cols), lambda i: (jnp.clip(i - first_step, 0, last), 0))


def _ffn_qkv(x2, gpre, gpost, wg, wu, wd, gmix, win, later_weights):
    n, d = x2.shape
    n_steps = n // ROW_TILE
    own = [wg, wu, wd, win]
    assert all(w.shape[0] % (WEIGHT_STEPS * BF16_SUBLANES) == 0 for w in own)
    row = lambda w: pl.BlockSpec((ROW_TILE, w), lambda i: (jnp.maximum(i - WEIGHT_STEPS, 0), 0))
    own_specs = [pl.BlockSpec((w.shape[0] // WEIGHT_STEPS, w.shape[1]),
                              lambda i: (jnp.minimum(i, WEIGHT_STEPS - 1), 0)) for w in own]
    cast_specs = [_cast_block_spec(w.shape, n_steps, WEIGHT_STEPS) for w in later_weights]
    outs = pl.pallas_call(
        functools.partial(_ffn_qkv_kernel, len(later_weights)),
        grid=(WEIGHT_STEPS + n_steps,),
        in_specs=[row(d), _resident((1, d)), _resident((1, d)), *own_specs[:3],
                  _resident((1, d)), own_specs[3]] + cast_specs,
        out_specs=[row(d), row(win.shape[1])] + cast_specs,
        out_shape=[jax.ShapeDtypeStruct((n, d), F32),
                   jax.ShapeDtypeStruct((n, win.shape[1]), BF16)]
                  + [jax.ShapeDtypeStruct(w.shape, BF16) for w in later_weights],
        scratch_shapes=[pltpu.VMEM(w.shape, BF16) for w in own],
        compiler_params=pltpu.CompilerParams(
            dimension_semantics=("arbitrary",), vmem_limit_bytes=DENSE_VMEM_BYTES),
        name="ffn1_qkv",
    )(x2, gpre, gpost, wg, wu, wd, gmix, win, *later_weights)
    return outs[0], outs[1], outs[2:]


def _stack_heads(q2):
    lane = lax.broadcasted_iota(jnp.int32, q2.shape, 1)
    zero = jnp.zeros_like(q2)
    qs = q2 * jnp.asarray(SCALE, q2.dtype)
    return jnp.concatenate([jnp.where(lane < HEAD_DIM, qs, zero),
                            jnp.where(lane >= HEAD_DIM, qs, zero)], axis=0)


_NT = (((1,), (1,)), ((), ()))


class _Chain:
    def __init__(self, n_units, stages, skewed):
        self.state = [None] * n_units
        if skewed:
            order = [(s - k, k) for s in range(n_units + len(stages) - 1)
                     for k in reversed(range(len(stages))) if 0 <= s - k < n_units]
        else:
            order = [(u, k) for k in range(len(stages)) for u in range(n_units)]
        self.steps = [functools.partial(self._apply, stages[k], u) for u, k in order]

    def _apply(self, stage, u):
        self.state[u] = stage(u, self.state[u])


def _emit_merged(*step_lists):
    tagged = sorted(((i + 0.5) / len(steps), n, i)
                    for n, steps in enumerate(step_lists) for i in range(len(steps)))
    for _, n, i in tagged:
        step_lists[n][i]()


def _stickbreak_tile(q_ref, k_ref, v_ref, o_ref, carry_ref, qi, other_steps):
    t = SB_TILE
    npair = W_SB // LANES
    pair_cols = [slice(hp * LANES, (hp + 1) * LANES) for hp in range(npair)]

    row = lax.broadcasted_iota(jnp.int32, (2 * t, t), 0)
    col = lax.broadcasted_iota(jnp.int32, (2 * t, t), 1)
    neg_after = jnp.where(lax.broadcasted_iota(jnp.int32, (t, t), 0)
                          > lax.broadcasted_iota(jnp.int32, (t, t), 1), -1.0, 0.0).astype(BF16)
    before = col < (row & (t - 1))
    first_head = lax.broadcasted_iota(jnp.int32, (t, LANES), 1) < HEAD_DIM
    q_stacks = [_stack_heads(q_ref[0, :, c]) for c in pair_cols]

    def sweep(kb, diag, gate=None, rows=(0, t)):
        start = pl.multiple_of(kb * t, t)
        r0, nr = rows
        assert not (diag and nr != t)
        heads_rows = (slice(r0, r0 + nr), slice(t + r0, t + r0 + nr))

        def scores(hp, _):
            q = q_stacks[hp]
            if nr != t:
                q = jnp.concatenate([q[s] for s in heads_rows], axis=0)
            return lax.dot_general(q, k_ref[0, pl.ds(start, t), pair_cols[hp]], _NT,
                                   preferred_element_type=F32)

        def split(hp, z):
            sp = jnp.maximum(z, 0.0) + jnp.log(1.0 + jnp.exp(-jnp.abs(z)))
            if diag:
                sp = jnp.where(before, sp, 0.0)
            return z - sp, sp[:, 0:1], sp.astype(BF16)

        def cumsum(hp, st):
            return st[0], st[1], jnp.dot(st[2], neg_after, preferred_element_type=F32)

        def weights(hp, st):
            log_sig, sp_first, c = st
            p = jnp.exp(log_sig + c)
            if diag:
                p = jnp.where(before, p, 0.0)
            return c[:, 0:1] - sp_first, p.astype(BF16)

        def values(hp, st):
            return st[0], jnp.dot(st[1], v_ref[0, pl.ds(start, t), pair_cols[hp]],
                                  preferred_element_type=F32)

        def update(hp, st):
            tot, pv = st
            if diag:
                new = tot
                o_ref[0, :, pair_cols[hp]] = jnp.where(first_head, pv[:t], pv[t:])
            else:
                old = jnp.concatenate([carry_ref[hp, s] for s in heads_rows], axis=0)
                scale = jnp.exp(old) if gate is None else jnp.exp(old) * gate
                pv, new = scale * pv, old + tot
                first = lax.broadcasted_iota(jnp.int32, (nr, LANES), 1) < HEAD_DIM
                o_ref[0, heads_rows[0], pair_cols[hp]] += jnp.where(first, pv[:nr], pv[nr:])
            carry_ref[hp, heads_rows[0]] = new[:nr]
            carry_ref[hp, heads_rows[1]] = new[nr:]
            return new

        return _Chain(npair, [scores, split, cumsum, weights, values, update], skewed=True)

    def worst_carry(carries, rows=None):
        worst = None
        for c in carries:
            parts = [c] if rows is None else [c[rows[0]:sum(rows)], c[t + rows[0]:t + sum(rows)]]
            for part in parts:
                worst = part if worst is None else jnp.maximum(worst, part)
        return jnp.max(worst)

    near, far = (0, SB_NEAR_ROWS), (SB_NEAR_ROWS, t - SB_NEAR_ROWS)
    diagonal = sweep(qi, True)
    previous = sweep(jnp.maximum(qi - 1, 0), False, gate=(qi > 0).astype(F32), rows=near)
    _emit_merged(diagonal.steps + previous.steps, other_steps)

    worst_near = worst_carry(previous.state)
    worst_far = worst_carry(diagonal.state, far)

    @pl.when(jnp.logical_and(qi > 0, worst_far >= EXP_ZERO_BELOW))
    def _():
        _emit_merged(sweep(qi - 1, False, rows=far).steps)

    def cond(st):
        return jnp.logical_and(st[0] < qi, st[1] >= EXP_ZERO_BELOW)

    def body(st):
        chain = sweep(qi - 1 - st[0], False)
        _emit_merged(chain.steps)
        return st[0] + 1, worst_carry(chain.state)

    lax.while_loop(cond, body, (jnp.int32(1), jnp.maximum(worst_near, worst_far)))


def _bias_by_offset(rel_bias):
    h = rel_bias.shape[0]
    rb = rel_bias.astype(F32)
    far = LOOKBACK * CHUNK + CH_TQ - 1 - REL_CLIP
    near = CH_EXT - far - (2 * REL_CLIP + 1)
    ext = jnp.concatenate([jnp.broadcast_to(rb[:, -1:], (h, far)), rb[:, ::-1],
                           jnp.broadcast_to(rb[:, :1], (h, near))], axis=1)
    return ext.reshape(h, 1, CH_EXT)


def _build_bias_table(ext_ref, tab_ref):
    shape = (CH_TQ, CH_EXT)
    i = lax.broadcasted_iota(jnp.int32, shape, 0)
    j = lax.broadcasted_iota(jnp.int32, shape, 1)
    dchunk = j // CHUNK - i // CHUNK
    valid = (dchunk >= 0) & (dchunk <= LOOKBACK) & (j < CH_WIN)
    for h in range(H_CH):
        rolled = pltpu.roll(jnp.broadcast_to(ext_ref[h], shape), CH_EXT - (CH_TQ - 1), 1,
                            stride=1, stride_axis=0)
        tab = jnp.where(valid, rolled, NEG_INF)
        for o in range(CH_TAB_BLKS):
            tab_ref[h, o] = tab[:, o * LANES:(o + 1) * LANES]


def _band_tiles(q_ref, k_ref, v_ref, o_ref, tab_ref, qi):
    npair = W_CH // LANES
    first_head = lax.broadcasted_iota(jnp.int32, (CH_TQ, LANES), 1) < HEAD_DIM
    units = [(sub, hp) for sub in range(CH_SUB) for hp in range(npair)]
    rows = [slice(sub * CH_TQ, (sub + 1) * CH_TQ) for sub in range(CH_SUB)]
    cols = [slice(hp * LANES, (hp + 1) * LANES) for hp in range(npair)]
    starts, shifts = [], []
    for sub in range(CH_SUB):
        nominal = (qi * CH_SUB + sub) * CH_TQ - LOOKBACK * CHUNK
        start = pl.multiple_of(jnp.maximum(nominal, 0), LANES)
        starts.append(start)
        shifts.append((start - nominal) // LANES)

    def scores(u, _):
        sub, hp = units[u]
        return lax.dot_general(_stack_heads(q_ref[0, rows[sub], cols[hp]]),
                               k_ref[0, pl.ds(starts[sub], CH_WIN), cols[hp]],
                               _NT, preferred_element_type=F32)

    def softmax(u, z):
        sub, hp = units[u]
        bias = jnp.concatenate(
            [jnp.concatenate([tab_ref[2 * hp + h, shifts[sub] + o] for o in range(CH_NBLK)],
                             axis=-1) for h in range(2)], axis=0)
        z = z + bias
        m = jnp.max(z, axis=-1, keepdims=True)
        p = jnp.exp(z - m)
        return jnp.sum(p, axis=-1, keepdims=True), p.astype(BF16)

    def values(u, st):
        sub, hp = units[u]
        return st[0], jnp.dot(st[1], v_ref[0, pl.ds(starts[sub], CH_WIN), cols[hp]],
                              preferred_element_type=F32)

    def store(u, st):
        sub, hp = units[u]
        out = st[1] / st[0]
        o_ref[0, rows[sub], cols[hp]] = jnp.where(first_head, out[:CH_TQ], out[CH_TQ:])

    return _Chain(len(units), [scores, softmax, values, store], skewed=False).steps


def _attn_kernel(ext_ref, qa_ref, ka_ref, va_ref, qb_ref, kb_ref, vb_ref, oa_ref, ob_ref,
                 carry_ref, tab_ref):
    @pl.when((pl.program_id(0) == 0) & (pl.program_id(1) == 0))
    def _():
        _build_bias_table(ext_ref, tab_ref)

    qi = pl.program_id(1)
    _stickbreak_tile(qa_ref, ka_ref, va_ref, oa_ref, carry_ref, qi,
                     _band_tiles(qb_ref, kb_ref, vb_ref, ob_ref, tab_ref, qi))


def _attention(qkv, ext, b, s):
    assert W_SB == W_CH and SB_TILE == CH_SUB * CH_TQ
    t, w = SB_TILE, W_SB
    tile = lambda c: pl.BlockSpec((1, t, w), lambda bi, qi: (bi, qi, c))
    whole = lambda c: pl.BlockSpec((1, s, w), lambda bi, qi: (bi, 0, c))
    return pl.pallas_call(
        _attn_kernel,
        grid=(b, s // t),
        in_specs=[_resident(ext.shape), tile(0), whole(1), whole(2), tile(3), whole(4), whole(5)],
        out_specs=[tile(0), tile(0)],
        out_shape=[jax.ShapeDtypeStruct((b, s, w), F32)] * 2,
        scratch_shapes=[pltpu.VMEM((w // LANES, 2 * t, 1), F32),
                        pltpu.VMEM((H_CH, CH_TAB_BLKS, CH_TQ, LANES), F32)],
        compiler_params=pltpu.CompilerParams(
            dimension_semantics=("arbitrary", "arbitrary"),
            vmem_limit_bytes=ATTN_VMEM_BYTES),
        name="mixers",
    )(ext, *([qkv] * 6))


def _tail_kernel(h_ref, oa_ref, ob_ref, p_ref, gsb_ref, gch_ref, wout_ref, gmixpost_ref,
                 gpre_ref, gpost_ref, wg_ref, wu_ref, wd_ref, wple_ref, wgate_ref, gple_ref,
                 out_ref):
    rows = _sub_tiles()

    def mix_norm(r, _):
        return jnp.concatenate([_rms(oa_ref[rows[r], :], gsb_ref[...]).astype(BF16),
                                _rms(ob_ref[rows[r], :], gch_ref[...]).astype(BF16)], axis=-1)

    def out_proj(r, mixed):
        y = jnp.dot(mixed, wout_ref[...], preferred_element_type=F32)
        proj = jnp.dot(p_ref[rows[r], :].astype(BF16), wple_ref[...], preferred_element_type=F32)
        return y, proj

    def mix_residual(r, st):
        h = h_ref[rows[r], :] + _rms(st[0], gmixpost_ref[...])
        return h, st[1], _rms(h, gpre_ref[...]).astype(BF16)

    def ffn(r, st):
        return st[0], st[1], _swiglu(st[2], wg_ref, wu_ref, wd_ref)

    def ffn_residual(r, st):
        h = st[0] + 0.5 * _rms(st[2], gpost_ref[...])
        return h, st[1], h.astype(BF16)

    def ple_gate(r, st):
        return st[0], st[1], jnp.dot(st[2], wgate_ref[...], preferred_element_type=F32)

    def ple_residual(r, st):
        h, proj, gate = st
        e = proj * jax.nn.sigmoid(gate)
        out_ref[rows[r], :] = h + _rms(e, gple_ref[...])

    _emit_merged(_Chain(ROW_SPLIT, [mix_norm, out_proj, mix_residual, ffn, ffn_residual,
                                    ple_gate, ple_residual], skewed=True).steps)


def _tail(h, oa, ob, p2, gsb, gch, wout, gmixpost, gpre, gpost, wg, wu, wd, wple, wgate, gple):
    n, d = h.shape
    d_ff = wg.shape[1]
    row = lambda w: pl.BlockSpec((ROW_TILE, w), lambda i: (i, 0))
    return pl.pallas_call(
        _tail_kernel,
        grid=(n // ROW_TILE,),
        in_specs=[row(d), row(oa.shape[1]), row(ob.shape[1]), row(p2.shape[1]),
                  _resident((1, oa.shape[1])), _resident((1, ob.shape[1])),
                  _resident(wout.shape), _resident((1, d)),
                  _resident((1, d)), _resident((1, d)),
                  _resident((d, d_ff)), _resident((d, d_ff)), _resident((d_ff, d)),
                  _resident(wple.shape), _resident(wgate.shape), _resident((1, d))],
        out_specs=row(d),
        out_shape=jax.ShapeDtypeStruct((n, d), F32),
        compiler_params=pltpu.CompilerParams(
            dimension_semantics=("arbitrary",), vmem_limit_bytes=DENSE_VMEM_BYTES),
        name="outproj_ffn2_ple",
    )(h, oa, ob, p2, gsb, gch, wout, gmixpost, gpre, gpost, wg, wu, wd, wple, wgate, gple)


def kernel(x, p, g_ffn1_pre, g_ffn1_post, w_ffn1_gate, w_ffn1_up, w_ffn1_down, g_mix_pre, g_mix_post, w_in, g_out_sb, g_out_ch, rel_bias, w_out, g_ffn2_pre, g_ffn2_post, w_ffn2_gate, w_ffn2_up, w_ffn2_down, w_ple_proj, w_ple_gate, g_ple_post):
    b, s, d = x.shape
    depth = p.shape[0]
    n = b * s
    gain = lambda g: g.astype(F32).reshape(1, -1)
    h = x.reshape(n, d)
    for i in range(depth):
        later = [w_out[i], w_ffn2_gate[i], w_ffn2_up[i], w_ffn2_down[i],
                 w_ple_proj[i], w_ple_gate[i]]
        h1, qkv, later_bf16 = _ffn_qkv(
            h, gain(g_ffn1_pre[i]), gain(g_ffn1_post[i]),
            w_ffn1_gate[i], w_ffn1_up[i], w_ffn1_down[i], gain(g_mix_pre[i]), w_in[i], later)
        wout, wg2, wu2, wd2, wple, wgate = later_bf16
        qkv3 = qkv.reshape(b, s, qkv.shape[1])
        o_a, o_b = _attention(qkv3, _bias_by_offset(rel_bias[i]), b, s)
        h = _tail(h1, o_a.reshape(n, W_SB), o_b.reshape(n, W_CH), p[i].reshape(n, -1),
                  gain(g_out_sb[i]), gain(g_out_ch[i]), wout, gain(g_mix_post[i]),
                  gain(g_ffn2_pre[i]), gain(g_ffn2_post[i]), wg2, wu2, wd2,
                  wple, wgate, gain(g_ple_post[i]))
    return h.reshape(b, s, d)
```

```python
import functools

import jax
import jax.numpy as jnp
from jax import lax
from jax.experimental import pallas as pl
from jax.experimental.pallas import tpu as pltpu

HEAD_DIM = 64
H_SB = 8
H_CH = 8
W_SB = H_SB * HEAD_DIM
W_CH = H_CH * HEAD_DIM
CHUNK = 64
LOOKBACK = 8
REL_CLIP = 128
EPS = 1e-6
NEG_INF = -1e30
SCALE = HEAD_DIM ** -0.5

LANES = 128
BF16_SUBLANES = 16
MXU_DIM = 256
ROW_TILE = 512
ROW_SPLIT = 2
WEIGHT_STEPS = 8
FF_SPLIT = 2
SB_TILE = 256
SB_NEAR_ROWS = 160
CH_TQ = 128
CH_WIN = LOOKBACK * CHUNK + CH_TQ
CH_NBLK = CH_WIN // LANES
CH_TAB_BLKS = CH_NBLK + LOOKBACK * CHUNK // LANES
CH_EXT = (CH_TAB_BLKS + 1) * LANES
CH_SUB = 2
EXP_ZERO_BELOW = -104.0
DENSE_VMEM_BYTES = 58 * 1024 * 1024
ATTN_VMEM_BYTES = 48 * 1024 * 1024

F32 = jnp.float32
BF16 = jnp.bfloat16


def _rms(x, g):
    ms = jnp.mean(x * x, axis=-1, keepdims=True)
    return x * lax.rsqrt(ms + EPS) * g


def _swiglu(u, wg_ref, wu_ref, wd_ref):
    d_ff = wg_ref.shape[1]
    tiles = d_ff // MXU_DIM
    edges = [((c * tiles + FF_SPLIT - 1) // FF_SPLIT) * MXU_DIM for c in range(FF_SPLIT)] + [d_ff]
    f = None
    for lo, hi in zip(edges[:-1], edges[1:]):
        gate = jnp.dot(u, wg_ref[:, lo:hi], preferred_element_type=F32)
        up = jnp.dot(u, wu_ref[:, lo:hi], preferred_element_type=F32)
        act = (gate * jax.nn.sigmoid(gate) * up).astype(BF16)
        part = jnp.dot(act, wd_ref[lo:hi, :], preferred_element_type=F32)
        f = part if f is None else f + part
    return f


def _ffn_qkv_kernel(n_cast, x_ref, gpre_ref, gpost_ref, wg_blk, wu_blk, wd_blk,
                    gmix_ref, win_blk, *rest):
    cast_in, (h_ref, qkv_ref) = rest[:n_cast], rest[n_cast:n_cast + 2]
    cast_out = rest[n_cast + 2:2 * n_cast + 2]
    wg_ref, wu_ref, wd_ref, win_ref = rest[2 * n_cast + 2:]
    step = pl.program_id(0)

    @pl.when(step < WEIGHT_STEPS)
    def _():
        for blk, dst in ((wg_blk, wg_ref), (wu_blk, wu_ref), (wd_blk, wd_ref), (win_blk, win_ref)):
            nr = blk.shape[0]
            dst[pl.ds(pl.multiple_of(step * nr, nr), nr), :] = blk[...].astype(BF16)

    @pl.when(step >= WEIGHT_STEPS)
    def _():
        _ffn_qkv_tile(x_ref, gpre_ref, gpost_ref, wg_ref, wu_ref, wd_ref, gmix_ref, win_ref,
                      cast_in, h_ref, qkv_ref, cast_out)


def _ffn_qkv_tile(x_ref, gpre_ref, gpost_ref, wg_ref, wu_ref, wd_ref, gmix_ref, win_ref,
                  cast_in, h_ref, qkv_ref, cast_out):
    rows = _sub_tiles()

    def prenorm(r, _):
        x = x_ref[rows[r], :]
        return x, _rms(x, gpre_ref[...]).astype(BF16)

    def ffn(r, st):
        return st[0], _swiglu(st[1], wg_ref, wu_ref, wd_ref)

    def residual(r, st):
        h = st[0] + 0.5 * _rms(st[1], gpost_ref[...])
        h_ref[rows[r], :] = h
        return _rms(h, gmix_ref[...]).astype(BF16)

    def project(r, u2):
        qkv_ref[rows[r], :] = jnp.dot(u2, win_ref[...], preferred_element_type=F32).astype(BF16)

    def cast_later_weights():
        for src, dst in zip(cast_in, cast_out):
            dst[...] = src[...].astype(BF16)

    steps = _Chain(ROW_SPLIT, [prenorm, ffn, residual, project], skewed=True).steps
    _emit_merged(steps[:2] + [cast_later_weights] + steps[2:])


def _sub_tiles():
    sub = ROW_TILE // ROW_SPLIT
    return [slice(r * sub, (r + 1) * sub) for r in range(ROW_SPLIT)]


def _resident(shape):
    nd = len(shape)
    return pl.BlockSpec(shape, lambda *_: (0,) * nd, pipeline_mode=pl.Buffered(1))


def _cast_block_spec(shape, n_steps, first_step):
    rows, cols = shape
    blk = next(r for r in range(BF16_SUBLANES, rows + 1, BF16_SUBLANES)
               if rows % r == 0 and r * n_steps >= rows)
    last = rows // blk - 1
    return pl.BlockSpec((blk, cols), lambda i: (jnp.clip(i - first_step, 0, last), 0))


def _ffn_qkv(x2, gpre, gpost, wg, wu, wd, gmix, win, later_weights):
    n, d = x2.shape
    n_steps = n // ROW_TILE
    own = [wg, wu, wd, win]
    assert all(w.shape[0] % (WEIGHT_STEPS * BF16_SUBLANES) == 0 for w in own)
    row = lambda w: pl.BlockSpec((ROW_TILE, w), lambda i: (jnp.maximum(i - WEIGHT_STEPS, 0), 0))
    own_specs = [pl.BlockSpec((w.shape[0] // WEIGHT_STEPS, w.shape[1]),
                              lambda i: (jnp.minimum(i, WEIGHT_STEPS - 1), 0)) for w in own]
    cast_specs = [_cast_block_spec(w.shape, n_steps, WEIGHT_STEPS) for w in later_weights]
    outs = pl.pallas_call(
        functools.partial(_ffn_qkv_kernel, len(later_weights)),
        grid=(WEIGHT_STEPS + n_steps,),
        in_specs=[row(d), _resident((1, d)), _resident((1, d)), *own_specs[:3],
                  _resident((1, d)), own_specs[3]] + cast_specs,
        out_specs=[row(d), row(win.shape[1])] + cast_specs,
        out_shape=[jax.ShapeDtypeStruct((n, d), F32),
                   jax.ShapeDtypeStruct((n, win.shape[1]), BF16)]
                  + [jax.ShapeDtypeStruct(w.shape, BF16) for w in later_weights],
        scratch_shapes=[pltpu.VMEM(w.shape, BF16) for w in own],
        compiler_params=pltpu.CompilerParams(
            dimension_semantics=("arbitrary",), vmem_limit_bytes=DENSE_VMEM_BYTES),
        name="ffn1_qkv",
    )(x2, gpre, gpost, wg, wu, wd, gmix, win, *later_weights)
    return outs[0], outs[1], outs[2:]


def _stack_heads(q2):
    lane = lax.broadcasted_iota(jnp.int32, q2.shape, 1)
    zero = jnp.zeros_like(q2)
    qs = q2 * jnp.asarray(SCALE, q2.dtype)
    return jnp.concatenate([jnp.where(lane < HEAD_DIM, qs, zero),
                            jnp.where(lane >= HEAD_DIM, qs, zero)], axis=0)


_NT = (((1,), (1,)), ((), ()))


class _Chain:
    def __init__(self, n_units, stages, skewed):
        self.state = [None] * n_units
        if skewed:
            order = [(s - k, k) for s in range(n_units + len(stages) - 1)
                     for k in reversed(range(len(stages))) if 0 <= s - k < n_units]
        else:
            order = [(u, k) for k in range(len(stages)) for u in range(n_units)]
        self.steps = [functools.partial(self._apply, stages[k], u) for u, k in order]

    def _apply(self, stage, u):
        self.state[u] = stage(u, self.state[u])


def _emit_merged(*step_lists):
    tagged = sorted(((i + 0.5) / len(steps), n, i)
                    for n, steps in enumerate(step_lists) for i in range(len(steps)))
    for _, n, i in tagged:
        step_lists[n][i]()


def _stickbreak_tile(q_ref, k_ref, v_ref, o_ref, carry_ref, qi, other_steps):
    t = SB_TILE
    npair = W_SB // LANES
    pair_cols = [slice(hp * LANES, (hp + 1) * LANES) for hp in range(npair)]

    neg_after = jnp.where(lax.broadcasted_iota(jnp.int32, (t, t), 0)
                          > lax.broadcasted_iota(jnp.int32, (t, t), 1), -1.0, 0.0).astype(BF16)
    q_stacks = [_stack_heads(q_ref[0, :, c]) for c in pair_cols]

    def sweep(blk, rows, causal=False, first=False, gate=None):
        r0, nr = rows
        start = pl.multiple_of(blk * t, t)
        heads_rows = (slice(r0, r0 + nr), slice(t + r0, t + r0 + nr))
        first_head = lax.broadcasted_iota(jnp.int32, (nr, LANES), 1) < HEAD_DIM
        if causal:
            ri = lax.broadcasted_iota(jnp.int32, (2 * nr, t), 0)
            ci = lax.broadcasted_iota(jnp.int32, (2 * nr, t), 1)
            before = ci < jnp.where(ri >= nr, ri - nr, ri) + r0

        def scores(hp, _):
            q = q_stacks[hp]
            if nr != t:
                q = jnp.concatenate([q[s] for s in heads_rows], axis=0)
            return lax.dot_general(q, k_ref[0, pl.ds(start, t), pair_cols[hp]], _NT,
                                   preferred_element_type=F32)

        def split(hp, z):
            sp = jnp.maximum(z, 0.0) + jnp.log(1.0 + jnp.exp(-jnp.abs(z)))
            if causal:
                sp = jnp.where(before, sp, 0.0)
            return z - sp, sp[:, 0:1], sp.astype(BF16)

        def cumsum(hp, st):
            return st[0], st[1], jnp.dot(st[2], neg_after, preferred_element_type=F32)

        def weights(hp, st):
            log_sig, sp_first, c = st
            p = jnp.exp(log_sig + c)
            if causal:
                p = jnp.where(before, p, 0.0)
            return c[:, 0:1] - sp_first, p.astype(BF16)

        def values(hp, st):
            return st[0], jnp.dot(st[1], v_ref[0, pl.ds(start, t), pair_cols[hp]],
                                  preferred_element_type=F32)

        def update(hp, st):
            tot, pv = st
            if first:
                new = tot
                o_ref[0, heads_rows[0], pair_cols[hp]] = jnp.where(first_head, pv[:nr], pv[nr:])
            else:
                old = jnp.concatenate([carry_ref[hp, s] for s in heads_rows], axis=0)
                scale = jnp.exp(old) if gate is None else jnp.exp(old) * gate
                pv, new = scale * pv, old + tot
                o_ref[0, heads_rows[0], pair_cols[hp]] += jnp.where(first_head, pv[:nr], pv[nr:])
            carry_ref[hp, heads_rows[0]] = new[:nr]
            carry_ref[hp, heads_rows[1]] = new[nr:]
            return new

        return _Chain(npair, [scores, split, cumsum, weights, values, update], skewed=True)

    def worst_carry(chain, rows, lo=None, hi=None):
        r0, nr = rows
        lo, hi = (r0 if lo is None else lo) - r0, (r0 + nr if hi is None else hi) - r0
        worst = None
        for c in chain.state:
            for part in (c[lo:hi], c[nr + lo:nr + hi]):
                worst = part if worst is None else jnp.maximum(worst, part)
        return jnp.max(worst)

    full, near, far = (0, t), (0, SB_NEAR_ROWS), (SB_NEAR_ROWS, t - SB_NEAR_ROWS)
    prev = jnp.maximum(qi - 1, 0)
    has_prev = qi > 0
    own = sweep(qi, full, causal=True, first=True)
    prev_near = sweep(prev, near, gate=has_prev.astype(F32))
    _emit_merged(own.steps + prev_near.steps, other_steps)

    worst_near = worst_carry(prev_near, near)
    worst_far = worst_carry(own, full, lo=SB_NEAR_ROWS)

    @pl.when(jnp.logical_and(has_prev, worst_far >= EXP_ZERO_BELOW))
    def _():
        _emit_merged(sweep(prev, far).steps)

    def cond(st):
        return jnp.logical_and(st[0] < qi, st[1] >= EXP_ZERO_BELOW)

    def body(st):
        chain = sweep(qi - 1 - st[0], full)
        _emit_merged(chain.steps)
        return st[0] + 1, worst_carry(chain, full)

    lax.while_loop(cond, body, (jnp.int32(1), jnp.maximum(worst_near, worst_far)))


def _bias_by_offset(rel_bias):
    h = rel_bias.shape[0]
    rb = rel_bias.astype(F32)
    far = LOOKBACK * CHUNK + CH_TQ - 1 - REL_CLIP
    near = CH_EXT - far - (2 * REL_CLIP + 1)
    ext = jnp.concatenate([jnp.broadcast_to(rb[:, -1:], (h, far)), rb[:, ::-1],
                           jnp.broadcast_to(rb[:, :1], (h, near))], axis=1)
    return ext.reshape(h, 1, CH_EXT)


def _build_bias_table(ext_ref, tab_ref):
    shape = (CH_TQ, CH_EXT)
    i = lax.broadcasted_iota(jnp.int32, shape, 0)
    j = lax.broadcasted_iota(jnp.int32, shape, 1)
    dchunk = j // CHUNK - i // CHUNK
    valid = (dchunk >= 0) & (dchunk <= LOOKBACK) & (j < CH_WIN)
    for h in range(H_CH):
        rolled = pltpu.roll(jnp.broadcast_to(ext_ref[h], shape), CH_EXT - (CH_TQ - 1), 1,
                            stride=1, stride_axis=0)
        tab = jnp.where(valid, rolled, NEG_INF)
        for o in range(CH_TAB_BLKS):
            tab_ref[h, o] = tab[:, o * LANES:(o + 1) * LANES]


def _band_tiles(q_ref, k_ref, v_ref, o_ref, tab_ref, qi):
    npair = W_CH // LANES
    first_head = lax.broadcasted_iota(jnp.int32, (CH_TQ, LANES), 1) < HEAD_DIM
    units = [(sub, hp) for sub in range(CH_SUB) for hp in range(npair)]
    rows = [slice(sub * CH_TQ, (sub + 1) * CH_TQ) for sub in range(CH_SUB)]
    cols = [slice(hp * LANES, (hp + 1) * LANES) for hp in range(npair)]
    starts, shifts = [], []
    for sub in range(CH_SUB):
        nominal = (qi * CH_SUB + sub) * CH_TQ - LOOKBACK * CHUNK
        start = pl.multiple_of(jnp.maximum(nominal, 0), LANES)
        starts.append(start)
        shifts.append((start - nominal) // LANES)

    def scores(u, _):
        sub, hp = units[u]
        return lax.dot_general(_stack_heads(q_ref[0, rows[sub], cols[hp]]),
                               k_ref[0, pl.ds(starts[sub], CH_WIN), cols[hp]],
                               _NT, preferred_element_type=F32)

    def softmax(u, z):
        sub, hp = units[u]
        bias = jnp.concatenate(
            [jnp.concatenate([tab_ref[2 * hp + h, shifts[sub] + o] for o in range(CH_NBLK)],
                             axis=-1) for h in range(2)], axis=0)
        z = z + bias
        m = jnp.max(z, axis=-1, keepdims=True)
        p = jnp.exp(z - m)
        return jnp.sum(p, axis=-1, keepdims=True), p.astype(BF16)

    def values(u, st):
        sub, hp = units[u]
        return st[0], jnp.dot(st[1], v_ref[0, pl.ds(starts[sub], CH_WIN), cols[hp]],
                              preferred_element_type=F32)

    def store(u, st):
        sub, hp = units[u]
        out = st[1] / st[0]
        o_ref[0, rows[sub], cols[hp]] = jnp.where(first_head, out[:CH_TQ], out[CH_TQ:])

    return _Chain(len(units), [scores, softmax, values, store], skewed=False).steps


def _attn_kernel(ext_ref, qa_ref, ka_ref, va_ref, qb_ref, kb_ref, vb_ref, oa_ref, ob_ref,
                 carry_ref, tab_ref):
    @pl.when((pl.program_id(0) == 0) & (pl.program_id(1) == 0))
    def _():
        _build_bias_table(ext_ref, tab_ref)

    qi = pl.program_id(1)
    _stickbreak_tile(qa_ref, ka_ref, va_ref, oa_ref, carry_ref, qi,
                     _band_tiles(qb_ref, kb_ref, vb_ref, ob_ref, tab_ref, qi))


def _attention(qkv, ext, b, s):
    assert W_SB == W_CH and SB_TILE == CH_SUB * CH_TQ
    t, w = SB_TILE, W_SB
    tile = lambda c: pl.BlockSpec((1, t, w), lambda bi, qi: (bi, qi, c))
    whole = lambda c: pl.BlockSpec((1, s, w), lambda bi, qi: (bi, 0, c))
    return pl.pallas_call(
        _attn_kernel,
        grid=(b, s // t),
        in_specs=[_resident(ext.shape), tile(0), whole(1), whole(2), tile(3), whole(4), whole(5)],
        out_specs=[tile(0), tile(0)],
        out_shape=[jax.ShapeDtypeStruct((b, s, w), F32)] * 2,
        scratch_shapes=[pltpu.VMEM((w // LANES, 2 * t, 1), F32),
                        pltpu.VMEM((H_CH, CH_TAB_BLKS, CH_TQ, LANES), F32)],
        compiler_params=pltpu.CompilerParams(
            dimension_semantics=("arbitrary", "arbitrary"),
            vmem_limit_bytes=ATTN_VMEM_BYTES),
        name="mixers",
    )(ext, *([qkv] * 6))


def _tail_kernel(h_ref, oa_ref, ob_ref, p_ref, gsb_ref, gch_ref, wout_ref, gmixpost_ref,
                 gpre_ref, gpost_ref, wg_ref, wu_ref, wd_ref, wple_ref, wgate_ref, gple_ref,
                 out_ref):
    rows = _sub_tiles()

    def mix_norm(r, _):
        return jnp.concatenate([_rms(oa_ref[rows[r], :], gsb_ref[...]).astype(BF16),
                                _rms(ob_ref[rows[r], :], gch_ref[...]).astype(BF16)], axis=-1)

    def out_proj(r, mixed):
        y = jnp.dot(mixed, wout_ref[...], preferred_element_type=F32)
        proj = jnp.dot(p_ref[rows[r], :].astype(BF16), wple_ref[...], preferred_element_type=F32)
        return y, proj

    def mix_residual(r, st):
        h = h_ref[rows[r], :] + _rms(st[0], gmixpost_ref[...])
        return h, st[1], _rms(h, gpre_ref[...]).astype(BF16)

    def ffn(r, st):
        return st[0], st[1], _swiglu(st[2], wg_ref, wu_ref, wd_ref)

    def ffn_residual(r, st):
        h = st[0] + 0.5 * _rms(st[2], gpost_ref[...])
        return h, st[1], h.astype(BF16)

    def ple_gate(r, st):
        return st[0], st[1], jnp.dot(st[2], wgate_ref[...], preferred_element_type=F32)

    def ple_residual(r, st):
        h, proj, gate = st
        e = proj * jax.nn.sigmoid(gate)
        out_ref[rows[r], :] = h + _rms(e, gple_ref[...])

    _emit_merged(_Chain(ROW_SPLIT, [mix_norm, out_proj, mix_residual, ffn, ffn_residual,
                                    ple_gate, ple_residual], skewed=True).steps)


def _tail(h, oa, ob, p2, gsb, gch, wout, gmixpost, gpre, gpost, wg, wu, wd, wple, wgate, gple):
    n, d = h.shape
    d_ff = wg.shape[1]
    row = lambda w: pl.BlockSpec((ROW_TILE, w), lambda i: (i, 0))
    return pl.pallas_call(
        _tail_kernel,
        grid=(n // ROW_TILE,),
        in_specs=[row(d), row(oa.shape[1]), row(ob.shape[1]), row(p2.shape[1]),
                  _resident((1, oa.shape[1])), _resident((1, ob.shape[1])),
                  _resident(wout.shape), _resident((1, d)),
                  _resident((1, d)), _resident((1, d)),
                  _resident((d, d_ff)), _resident((d, d_ff)), _resident((d_ff, d)),
                  _resident(wple.shape), _resident(wgate.shape), _resident((1, d))],
        out_specs=row(d),
        out_shape=jax.ShapeDtypeStruct((n, d), F32),
        compiler_params=pltpu.CompilerParams(
            dimension_semantics=("arbitrary",), vmem_limit_bytes=DENSE_VMEM_BYTES),
        name="outproj_ffn2_ple",
    )(h, oa, ob, p2, gsb, gch, wout, gmixpost, gpre, gpost, wg, wu, wd, wple, wgate, gple)


def kernel(x, p, g_ffn1_pre, g_ffn1_post, w_ffn1_gate, w_ffn1_up, w_ffn1_down, g_mix_pre, g_mix_post, w_in, g_out_sb, g_out_ch, rel_bias, w_out, g_ffn2_pre, g_ffn2_post, w_ffn2_gate, w_ffn2_up, w_ffn2_down, w_ple_proj, w_ple_gate, g_ple_post):
    b, s, d = x.shape
    depth = p.shape[0]
    n = b * s
    gain = lambda g: g.astype(F32).reshape(1, -1)
    h = x.reshape(n, d)
    for i in range(depth):
        later = [w_out[i], w_ffn2_gate[i], w_ffn2_up[i], w_ffn2_down[i],
                 w_ple_proj[i], w_ple_gate[i]]
        h1, qkv, later_bf16 = _ffn_qkv(
            h, gain(g_ffn1_pre[i]), gain(g_ffn1_post[i]),
            w_ffn1_gate[i], w_ffn1_up[i], w_ffn1_down[i], gain(g_mix_pre[i]), w_in[i], later)
        wout, wg2, wu2, wd2, wple, wgate = later_bf16
        qkv3 = qkv.reshape(b, s, qkv.shape[1])
        o_a, o_b = _attention(qkv3, _bias_by_offset(rel_bias[i]), b, s)
        h = _tail(h1, o_a.reshape(n, W_SB), o_b.reshape(n, W_CH), p[i].reshape(n, -1),
                  gain(g_out_sb[i]), gain(g_out_ch[i]), wout, gain(g_mix_post[i]),
                  gain(g_ffn2_pre[i]), gain(g_ffn2_post[i]), wg2, wu2, wd2,
                  wple, wgate, gain(g_ple_post[i]))
    return h.reshape(b, s, d)
```

```python
import functools

import jax
import jax.numpy as jnp
from jax import lax
from jax.experimental import pallas as pl
from jax.experimental.pallas import tpu as pltpu

HEAD_DIM = 64
H_SB = 8
H_CH = 8
W_SB = H_SB * HEAD_DIM
W_CH = H_CH * HEAD_DIM
CHUNK = 64
LOOKBACK = 8
REL_CLIP = 128
EPS = 1e-6
NEG_INF = -1e30
SCALE = HEAD_DIM ** -0.5

LANES = 128
BF16_SUBLANES = 16
MXU_DIM = 256
ROW_TILE = 512
ROW_SPLIT = 2
WEIGHT_STEPS = 8
FF_SPLIT = 2
SB_TILE = 256
SB_NEAR_ROWS = 160
CH_TQ = 128
CH_WIN = LOOKBACK * CHUNK + CH_TQ
CH_NBLK = CH_WIN // LANES
CH_TAB_BLKS = CH_NBLK + LOOKBACK * CHUNK // LANES
CH_EXT = (CH_TAB_BLKS + 1) * LANES
CH_SUB = 2
EXP_ZERO_BELOW = -104.0
DENSE_VMEM_BYTES = 58 * 1024 * 1024
ATTN_VMEM_BYTES = 48 * 1024 * 1024

F32 = jnp.float32
BF16 = jnp.bfloat16


def _rms(x, g):
    ms = jnp.mean(x * x, axis=-1, keepdims=True)
    return x * lax.rsqrt(ms + EPS) * g


def _swiglu(u, wg_ref, wu_ref, wd_ref):
    d_ff = wg_ref.shape[1]
    tiles = d_ff // MXU_DIM
    edges = [((c * tiles + FF_SPLIT - 1) // FF_SPLIT) * MXU_DIM for c in range(FF_SPLIT)] + [d_ff]
    f = None
    for lo, hi in zip(edges[:-1], edges[1:]):
        gate = jnp.dot(u, wg_ref[:, lo:hi], preferred_element_type=F32)
        up = jnp.dot(u, wu_ref[:, lo:hi], preferred_element_type=F32)
        act = (gate * jax.nn.sigmoid(gate) * up).astype(BF16)
        part = jnp.dot(act, wd_ref[lo:hi, :], preferred_element_type=F32)
        f = part if f is None else f + part
    return f


def _ffn_qkv_kernel(n_cast, x_ref, gpre_ref, gpost_ref, wg_blk, wu_blk, wd_blk,
                    gmix_ref, win_blk, *rest):
    cast_in, (h_ref, qkv_ref, vt_ref) = rest[:n_cast], rest[n_cast:n_cast + 3]
    cast_out = rest[n_cast + 3:2 * n_cast + 3]
    wg_ref, wu_ref, wd_ref, win_ref = rest[2 * n_cast + 3:]
    step = pl.program_id(0)

    @pl.when(step < WEIGHT_STEPS)
    def _():
        for blk, dst in ((wg_blk, wg_ref), (wu_blk, wu_ref), (wd_blk, wd_ref), (win_blk, win_ref)):
            nr = blk.shape[0]
            dst[pl.ds(pl.multiple_of(step * nr, nr), nr), :] = blk[...].astype(BF16)

    @pl.when(step >= WEIGHT_STEPS)
    def _():
        _ffn_qkv_tile(x_ref, gpre_ref, gpost_ref, wg_ref, wu_ref, wd_ref, gmix_ref, win_ref,
                      cast_in, h_ref, qkv_ref, vt_ref, cast_out)


def _ffn_qkv_tile(x_ref, gpre_ref, gpost_ref, wg_ref, wu_ref, wd_ref, gmix_ref, win_ref,
                  cast_in, h_ref, qkv_ref, vt_ref, cast_out):
    for src, dst in zip(cast_in, cast_out):
        dst[...] = src[...].astype(BF16)
    rows = _sub_tiles()
    sub_blocks = ROW_TILE // ROW_SPLIT // LANES

    def prenorm(r, _):
        x = x_ref[rows[r], :]
        return x, _rms(x, gpre_ref[...]).astype(BF16)

    def ffn(r, st):
        return st[0], _swiglu(st[1], wg_ref, wu_ref, wd_ref)

    def residual(r, st):
        h = st[0] + 0.5 * _rms(st[1], gpost_ref[...])
        h_ref[rows[r], :] = h
        return _rms(h, gmix_ref[...]).astype(BF16)

    def project(r, u2):
        qkv = jnp.dot(u2, win_ref[...], preferred_element_type=F32)
        qkv_ref[rows[r], :] = qkv.astype(BF16)
        v_band = qkv[:, qkv.shape[1] - W_CH:]
        for j in range(sub_blocks):
            vt_ref[r * sub_blocks + j] = v_band[j * LANES:(j + 1) * LANES, :].T.astype(BF16)

    _emit_merged(_Chain(ROW_SPLIT, [prenorm, ffn, residual, project], skewed=True).steps)


def _sub_tiles():
    sub = ROW_TILE // ROW_SPLIT
    return [slice(r * sub, (r + 1) * sub) for r in range(ROW_SPLIT)]


def _resident(shape):
    nd = len(shape)
    return pl.BlockSpec(shape, lambda *_: (0,) * nd, pipeline_mode=pl.Buffered(1))


def _cast_block_spec(shape, n_steps, first_step):
    rows, cols = shape
    blk = next(r for r in range(BF16_SUBLANES, rows + 1, BF16_SUBLANES)
               if rows % r == 0 and r * n_steps >= rows)
    last = rows // blk - 1
    return pl.BlockSpec((blk, cols), lambda i: (jnp.clip(i - first_step, 0, last), 0))


def _ffn_qkv(x2, gpre, gpost, wg, wu, wd, gmix, win, later_weights):
    n, d = x2.shape
    n_steps = n // ROW_TILE
    own = [wg, wu, wd, win]
    assert all(w.shape[0] % (WEIGHT_STEPS * BF16_SUBLANES) == 0 for w in own)
    tile_index = lambda i: jnp.maximum(i - WEIGHT_STEPS, 0)
    row = lambda w: pl.BlockSpec((ROW_TILE, w), lambda i: (tile_index(i), 0))
    vt_spec = pl.BlockSpec((ROW_TILE // LANES, W_CH, LANES), lambda i: (tile_index(i), 0, 0))
    own_specs = [pl.BlockSpec((w.shape[0] // WEIGHT_STEPS, w.shape[1]),
                              lambda i: (jnp.minimum(i, WEIGHT_STEPS - 1), 0)) for w in own]
    cast_specs = [_cast_block_spec(w.shape, n_steps, WEIGHT_STEPS) for w in later_weights]
    outs = pl.pallas_call(
        functools.partial(_ffn_qkv_kernel, len(later_weights)),
        grid=(WEIGHT_STEPS + n_steps,),
        in_specs=[row(d), _resident((1, d)), _resident((1, d)), *own_specs[:3],
                  _resident((1, d)), own_specs[3]] + cast_specs,
        out_specs=[row(d), row(win.shape[1]), vt_spec] + cast_specs,
        out_shape=[jax.ShapeDtypeStruct((n, d), F32),
                   jax.ShapeDtypeStruct((n, win.shape[1]), BF16),
                   jax.ShapeDtypeStruct((n // LANES, W_CH, LANES), BF16)]
                  + [jax.ShapeDtypeStruct(w.shape, BF16) for w in later_weights],
        scratch_shapes=[pltpu.VMEM(w.shape, BF16) for w in own],
        compiler_params=pltpu.CompilerParams(
            dimension_semantics=("arbitrary",), vmem_limit_bytes=DENSE_VMEM_BYTES),
        name="ffn1_qkv",
    )(x2, gpre, gpost, wg, wu, wd, gmix, win, *later_weights)
    return outs[0], outs[1], outs[2], outs[3:]


def _stack_heads(q2):
    lane = lax.broadcasted_iota(jnp.int32, q2.shape, 1)
    zero = jnp.zeros_like(q2)
    qs = q2 * jnp.asarray(SCALE, q2.dtype)
    return jnp.concatenate([jnp.where(lane < HEAD_DIM, qs, zero),
                            jnp.where(lane >= HEAD_DIM, qs, zero)], axis=0)


_NT = (((1,), (1,)), ((), ()))


class _Chain:
    def __init__(self, n_units, stages, skewed):
        self.state = [None] * n_units
        if skewed:
            order = [(s - k, k) for s in range(n_units + len(stages) - 1)
                     for k in reversed(range(len(stages))) if 0 <= s - k < n_units]
        else:
            order = [(u, k) for k in range(len(stages)) for u in range(n_units)]
        self.steps = [functools.partial(self._apply, stages[k], u) for u, k in order]

    def _apply(self, stage, u):
        self.state[u] = stage(u, self.state[u])


def _emit_merged(*step_lists):
    tagged = sorted(((i + 0.5) / len(steps), n, i)
                    for n, steps in enumerate(step_lists) for i in range(len(steps)))
    for _, n, i in tagged:
        step_lists[n][i]()


def _stickbreak_tile(q_ref, k_ref, v_ref, o_ref, carry_ref, qi, other_steps):
    t = SB_TILE
    npair = W_SB // LANES
    pair_cols = [slice(hp * LANES, (hp + 1) * LANES) for hp in range(npair)]

    neg_after = jnp.where(lax.broadcasted_iota(jnp.int32, (t, t), 0)
                          > lax.broadcasted_iota(jnp.int32, (t, t), 1), -1.0, 0.0).astype(BF16)
    q_stacks = [_stack_heads(q_ref[0, :, c]) for c in pair_cols]

    def sweep(blk, rows, causal=False, first=False, gate=None):
        r0, nr = rows
        start = pl.multiple_of(blk * t, t)
        heads_rows = (slice(r0, r0 + nr), slice(t + r0, t + r0 + nr))
        first_head = lax.broadcasted_iota(jnp.int32, (nr, LANES), 1) < HEAD_DIM
        if causal:
            ri = lax.broadcasted_iota(jnp.int32, (2 * nr, t), 0)
            ci = lax.broadcasted_iota(jnp.int32, (2 * nr, t), 1)
            before = ci < jnp.where(ri >= nr, ri - nr, ri) + r0

        def scores(hp, _):
            q = q_stacks[hp]
            if nr != t:
                q = jnp.concatenate([q[s] for s in heads_rows], axis=0)
            return lax.dot_general(q, k_ref[0, pl.ds(start, t), pair_cols[hp]], _NT,
                                   preferred_element_type=F32)

        def split(hp, z):
            sp = jnp.maximum(z, 0.0) + jnp.log(1.0 + jnp.exp(-jnp.abs(z)))
            if causal:
                sp = jnp.where(before, sp, 0.0)
            return z - sp, sp[:, 0:1], sp.astype(BF16)

        def cumsum(hp, st):
            return st[0], st[1], jnp.dot(st[2], neg_after, preferred_element_type=F32)

        def weights(hp, st):
            log_sig, sp_first, c = st
            p = jnp.exp(log_sig + c)
            if causal:
                p = jnp.where(before, p, 0.0)
            return c[:, 0:1] - sp_first, p.astype(BF16)

        def values(hp, st):
            return st[0], jnp.dot(st[1], v_ref[0, pl.ds(start, t), pair_cols[hp]],
                                  preferred_element_type=F32)

        def update(hp, st):
            tot, pv = st
            if first:
                new = tot
                o_ref[0, heads_rows[0], pair_cols[hp]] = jnp.where(first_head, pv[:nr], pv[nr:])
            else:
                old = jnp.concatenate([carry_ref[hp, s] for s in heads_rows], axis=0)
                scale = jnp.exp(old) if gate is None else jnp.exp(old) * gate
                pv, new = scale * pv, old + tot
                o_ref[0, heads_rows[0], pair_cols[hp]] += jnp.where(first_head, pv[:nr], pv[nr:])
            carry_ref[hp, heads_rows[0]] = new[:nr]
            carry_ref[hp, heads_rows[1]] = new[nr:]
            return new

        return _Chain(npair, [scores, split, cumsum, weights, values, update], skewed=True)

    def worst_carry(chain, rows, lo=None, hi=None):
        r0, nr = rows
        lo, hi = (r0 if lo is None else lo) - r0, (r0 + nr if hi is None else hi) - r0
        worst = None
        for c in chain.state:
            for part in (c[lo:hi], c[nr + lo:nr + hi]):
                worst = part if worst is None else jnp.maximum(worst, part)
        return jnp.max(worst)

    full, near, far = (0, t), (0, SB_NEAR_ROWS), (SB_NEAR_ROWS, t - SB_NEAR_ROWS)
    prev = jnp.maximum(qi - 1, 0)
    has_prev = qi > 0
    own = sweep(qi, full, causal=True, first=True)
    prev_near = sweep(prev, near, gate=has_prev.astype(F32))
    _emit_merged(own.steps + prev_near.steps, other_steps)

    worst_near = worst_carry(prev_near, near)
    worst_far = worst_carry(own, full, lo=SB_NEAR_ROWS)

    @pl.when(jnp.logical_and(has_prev, worst_far >= EXP_ZERO_BELOW))
    def _():
        _emit_merged(sweep(prev, far).steps)

    def cond(st):
        return jnp.logical_and(st[0] < qi, st[1] >= EXP_ZERO_BELOW)

    def body(st):
        chain = sweep(qi - 1 - st[0], full)
        _emit_merged(chain.steps)
        return st[0] + 1, worst_carry(chain, full)

    lax.while_loop(cond, body, (jnp.int32(1), jnp.maximum(worst_near, worst_far)))


def _bias_by_offset(rel_bias):
    h = rel_bias.shape[0]
    rb = rel_bias.astype(F32)
    far = LOOKBACK * CHUNK + CH_TQ - 1 - REL_CLIP
    near = CH_EXT - far - (2 * REL_CLIP + 1)
    ext = jnp.concatenate([jnp.broadcast_to(rb[:, -1:], (h, far)), rb[:, ::-1],
                           jnp.broadcast_to(rb[:, :1], (h, near))], axis=1)
    return ext.reshape(h, 1, CH_EXT)


def _build_bias_table(ext_ref, tab_ref):
    shape = (CH_TQ, CH_EXT)
    i = lax.broadcasted_iota(jnp.int32, shape, 0)
    j = lax.broadcasted_iota(jnp.int32, shape, 1)
    dchunk = j // CHUNK - i // CHUNK
    valid = (dchunk >= 0) & (dchunk <= LOOKBACK) & (j < CH_WIN)
    for h in range(H_CH):
        rolled = pltpu.roll(jnp.broadcast_to(ext_ref[h], shape), CH_EXT - (CH_TQ - 1), 1,
                            stride=1, stride_axis=0)
        tab = jnp.where(valid, rolled, NEG_INF)
        for o in range(CH_TAB_BLKS):
            tab_ref[h, o] = tab[:, o * LANES:(o + 1) * LANES].T


def _band_tiles(q_ref, k_ref, vt_ref, o_ref, tab_ref, qi):
    npair = W_CH // LANES
    units = [(sub, hp) for sub in range(CH_SUB) for hp in range(npair)]
    rows = [slice(sub * CH_TQ, (sub + 1) * CH_TQ) for sub in range(CH_SUB)]
    cols = [slice(hp * LANES, (hp + 1) * LANES) for hp in range(npair)]
    starts, shifts = [], []
    for sub in range(CH_SUB):
        nominal = (qi * CH_SUB + sub) * CH_TQ - LOOKBACK * CHUNK
        start = pl.multiple_of(jnp.maximum(nominal, 0), LANES)
        starts.append(start)
        shifts.append((start - nominal) // LANES)

    def scores(u, _):
        sub, hp = units[u]
        return lax.dot_general(k_ref[0, pl.ds(starts[sub], CH_WIN), cols[hp]],
                               _stack_heads(q_ref[0, rows[sub], cols[hp]]),
                               _NT, preferred_element_type=F32)

    def softmax(u, z):
        sub, hp = units[u]
        bias = jnp.concatenate(
            [jnp.concatenate([tab_ref[2 * hp + h, shifts[sub] + o] for h in range(2)], axis=-1)
             for o in range(CH_NBLK)], axis=0)
        z = z + bias
        m = jnp.max(z, axis=0, keepdims=True)
        p = jnp.exp(z - m)
        return jnp.sum(p, axis=0, keepdims=True), p.astype(BF16)

    def values(u, st):
        sub, hp = units[u]
        first_blk = starts[sub] // LANES
        vt = jnp.concatenate([vt_ref[0, first_blk + o, cols[hp], :] for o in range(CH_NBLK)],
                             axis=-1)
        return st[0], jnp.dot(vt, st[1], preferred_element_type=F32)

    def store(u, st):
        sub, hp = units[u]
        out = st[1] / st[0]
        own = jnp.concatenate([out[0:HEAD_DIM, 0:CH_TQ], out[HEAD_DIM:, CH_TQ:]], axis=0)
        o_ref[0, rows[sub], cols[hp]] = own.T

    return _Chain(len(units), [scores, softmax, values, store], skewed=False).steps


def _attn_kernel(ext_ref, qa_ref, ka_ref, va_ref, qb_ref, kb_ref, vb_ref, oa_ref, ob_ref,
                 carry_ref, tab_ref):
    @pl.when((pl.program_id(0) == 0) & (pl.program_id(1) == 0))
    def _():
        _build_bias_table(ext_ref, tab_ref)

    qi = pl.program_id(1)
    _stickbreak_tile(qa_ref, ka_ref, va_ref, oa_ref, carry_ref, qi,
                     _band_tiles(qb_ref, kb_ref, vb_ref, ob_ref, tab_ref, qi))


def _attention(qkv, vt_band, ext, b, s):
    assert W_SB == W_CH and SB_TILE == CH_SUB * CH_TQ
    t, w = SB_TILE, W_SB
    tile = lambda c: pl.BlockSpec((1, t, w), lambda bi, qi: (bi, qi, c))
    whole = lambda c: pl.BlockSpec((1, s, w), lambda bi, qi: (bi, 0, c))
    vt_spec = pl.BlockSpec((1,) + vt_band.shape[1:], lambda bi, qi: (bi, 0, 0, 0))
    return pl.pallas_call(
        _attn_kernel,
        grid=(b, s // t),
        in_specs=[_resident(ext.shape), tile(0), whole(1), whole(2), tile(3), whole(4), vt_spec],
        out_specs=[tile(0), tile(0)],
        out_shape=[jax.ShapeDtypeStruct((b, s, w), F32)] * 2,
        scratch_shapes=[pltpu.VMEM((w // LANES, 2 * t, 1), F32),
                        pltpu.VMEM((H_CH, CH_TAB_BLKS, LANES, CH_TQ), F32)],
        compiler_params=pltpu.CompilerParams(
            dimension_semantics=("arbitrary", "arbitrary"),
            vmem_limit_bytes=ATTN_VMEM_BYTES),
        name="mixers",
    )(ext, *([qkv] * 5), vt_band)


def _tail_kernel(h_ref, oa_ref, ob_ref, p_ref, gsb_ref, gch_ref, wout_ref, gmixpost_ref,
                 gpre_ref, gpost_ref, wg_ref, wu_ref, wd_ref, wple_ref, wgate_ref, gple_ref,
                 out_ref):
    rows = _sub_tiles()

    def mix_norm(r, _):
        return jnp.concatenate([_rms(oa_ref[rows[r], :], gsb_ref[...]).astype(BF16),
                                _rms(ob_ref[rows[r], :], gch_ref[...]).astype(BF16)], axis=-1)

    def out_proj(r, mixed):
        y = jnp.dot(mixed, wout_ref[...], preferred_element_type=F32)
        proj = jnp.dot(p_ref[rows[r], :].astype(BF16), wple_ref[...], preferred_element_type=F32)
        return y, proj

    def mix_residual(r, st):
        h = h_ref[rows[r], :] + _rms(st[0], gmixpost_ref[...])
        return h, st[1], _rms(h, gpre_ref[...]).astype(BF16)

    def ffn(r, st):
        return st[0], st[1], _swiglu(st[2], wg_ref, wu_ref, wd_ref)

    def ffn_residual(r, st):
        h = st[0] + 0.5 * _rms(st[2], gpost_ref[...])
        return h, st[1], h.astype(BF16)

    def ple_gate(r, st):
        return st[0], st[1], jnp.dot(st[2], wgate_ref[...], preferred_element_type=F32)

    def ple_residual(r, st):
        h, proj, gate = st
        e = proj * jax.nn.sigmoid(gate)
        out_ref[rows[r], :] = h + _rms(e, gple_ref[...])

    _emit_merged(_Chain(ROW_SPLIT, [mix_norm, out_proj, mix_residual, ffn, ffn_residual,
                                    ple_gate, ple_residual], skewed=True).steps)


def _tail(h, oa, ob, p2, gsb, gch, wout, gmixpost, gpre, gpost, wg, wu, wd, wple, wgate, gple):
    n, d = h.shape
    d_ff = wg.shape[1]
    row = lambda w: pl.BlockSpec((ROW_TILE, w), lambda i: (i, 0))
    return pl.pallas_call(
        _tail_kernel,
        grid=(n // ROW_TILE,),
        in_specs=[row(d), row(oa.shape[1]), row(ob.shape[1]), row(p2.shape[1]),
                  _resident((1, oa.shape[1])), _resident((1, ob.shape[1])),
                  _resident(wout.shape), _resident((1, d)),
                  _resident((1, d)), _resident((1, d)),
                  _resident((d, d_ff)), _resident((d, d_ff)), _resident((d_ff, d)),
                  _resident(wple.shape), _resident(wgate.shape), _resident((1, d))],
        out_specs=row(d),
        out_shape=jax.ShapeDtypeStruct((n, d), F32),
        compiler_params=pltpu.CompilerParams(
            dimension_semantics=("arbitrary",), vmem_limit_bytes=DENSE_VMEM_BYTES),
        name="outproj_ffn2_ple",
    )(h, oa, ob, p2, gsb, gch, wout, gmixpost, gpre, gpost, wg, wu, wd, wple, wgate, gple)


def kernel(x, p, g_ffn1_pre, g_ffn1_post, w_ffn1_gate, w_ffn1_up, w_ffn1_down, g_mix_pre, g_mix_post, w_in, g_out_sb, g_out_ch, rel_bias, w_out, g_ffn2_pre, g_ffn2_post, w_ffn2_gate, w_ffn2_up, w_ffn2_down, w_ple_proj, w_ple_gate, g_ple_post):
    b, s, d = x.shape
    depth = p.shape[0]
    n = b * s
    gain = lambda g: g.astype(F32).reshape(1, -1)
    h = x.reshape(n, d)
    for i in range(depth):
        later = [w_out[i], w_ffn2_gate[i], w_ffn2_up[i], w_ffn2_down[i],
                 w_ple_proj[i], w_ple_gate[i]]
        h1, qkv, vt_band, later_bf16 = _ffn_qkv(
            h, gain(g_ffn1_pre[i]), gain(g_ffn1_post[i]),
            w_ffn1_gate[i], w_ffn1_up[i], w_ffn1_down[i], gain(g_mix_pre[i]), w_in[i], later)
        wout, wg2, wu2, wd2, wple, wgate = later_bf16
        qkv3 = qkv.reshape(b, s, qkv.shape[1])
        vt_band = vt_band.reshape(b, s // LANES, W_CH, LANES)
        o_a, o_b = _attention(qkv3, vt_band, _bias_by_offset(rel_bias[i]), b, s)
        h = _tail(h1, o_a.reshape(n, W_SB), o_b.reshape(n, W_CH), p[i].reshape(n, -1),
                  gain(g_out_sb[i]), gain(g_out_ch[i]), wout, gain(g_mix_post[i]),
                  gain(g_ffn2_pre[i]), gain(g_ffn2_post[i]), wg2, wu2, wd2,
                  wple, wgate, gain(g_ple_post[i]))
    return h.reshape(b, s, d)
```

```python
import functools

import jax
import jax.numpy as jnp
from jax import lax
from jax.experimental import pallas as pl
from jax.experimental.pallas import tpu as pltpu

HEAD_DIM = 64
H_SB = 8
H_CH = 8
W_SB = H_SB * HEAD_DIM
W_CH = H_CH * HEAD_DIM
CHUNK = 64
LOOKBACK = 8
REL_CLIP = 128
EPS = 1e-6
NEG_INF = -1e30
SCALE = HEAD_DIM ** -0.5

LANES = 128
BF16_SUBLANES = 16
MXU_DIM = 256
ROW_TILE = 512
ROW_SPLIT = 2
WEIGHT_STEPS = 8
FF_SPLIT = 2
SB_TILE = 256
SB_NEAR_ROWS = 160
CH_TQ = 128
CH_WIN = LOOKBACK * CHUNK + CH_TQ
CH_NBLK = CH_WIN // LANES
CH_TAB_BLKS = CH_NBLK + LOOKBACK * CHUNK // LANES
CH_EXT = (CH_TAB_BLKS + 1) * LANES
CH_SUB = 2
MIX_TILES = 2
EXP_ZERO_BELOW = -104.0
DENSE_VMEM_BYTES = 58 * 1024 * 1024
ATTN_VMEM_BYTES = 48 * 1024 * 1024

F32 = jnp.float32
BF16 = jnp.bfloat16


def _rms(x, g):
    ms = jnp.mean(x * x, axis=-1, keepdims=True)
    return x * lax.rsqrt(ms + EPS) * g


def _swiglu(u, wg_ref, wu_ref, wd_ref):
    d_ff = wg_ref.shape[1]
    tiles = d_ff // MXU_DIM
    edges = [((c * tiles + FF_SPLIT - 1) // FF_SPLIT) * MXU_DIM for c in range(FF_SPLIT)] + [d_ff]
    f = None
    for lo, hi in zip(edges[:-1], edges[1:]):
        gate = jnp.dot(u, wg_ref[:, lo:hi], preferred_element_type=F32)
        up = jnp.dot(u, wu_ref[:, lo:hi], preferred_element_type=F32)
        act = (gate * jax.nn.sigmoid(gate) * up).astype(BF16)
        part = jnp.dot(act, wd_ref[lo:hi, :], preferred_element_type=F32)
        f = part if f is None else f + part
    return f


def _ffn_qkv_kernel(n_cast, x_ref, gpre_ref, gpost_ref, wg_blk, wu_blk, wd_blk,
                    gmix_ref, win_blk, *rest):
    cast_in, (h_ref, qkv_ref, vt_ref) = rest[:n_cast], rest[n_cast:n_cast + 3]
    cast_out = rest[n_cast + 3:2 * n_cast + 3]
    wg_ref, wu_ref, wd_ref, win_ref = rest[2 * n_cast + 3:]
    step = pl.program_id(0)

    @pl.when(step < WEIGHT_STEPS)
    def _():
        for blk, dst in ((wg_blk, wg_ref), (wu_blk, wu_ref), (wd_blk, wd_ref), (win_blk, win_ref)):
            nr = blk.shape[0]
            dst[pl.ds(pl.multiple_of(step * nr, nr), nr), :] = blk[...].astype(BF16)

    @pl.when(step >= WEIGHT_STEPS)
    def _():
        _ffn_qkv_tile(x_ref, gpre_ref, gpost_ref, wg_ref, wu_ref, wd_ref, gmix_ref, win_ref,
                      cast_in, h_ref, qkv_ref, vt_ref, cast_out)


def _ffn_qkv_tile(x_ref, gpre_ref, gpost_ref, wg_ref, wu_ref, wd_ref, gmix_ref, win_ref,
                  cast_in, h_ref, qkv_ref, vt_ref, cast_out):
    for src, dst in zip(cast_in, cast_out):
        dst[...] = src[...].astype(BF16)
    rows = _sub_tiles()
    sub_blocks = ROW_TILE // ROW_SPLIT // LANES

    def prenorm(r, _):
        x = x_ref[rows[r], :]
        return x, _rms(x, gpre_ref[...]).astype(BF16)

    def ffn(r, st):
        return st[0], _swiglu(st[1], wg_ref, wu_ref, wd_ref)

    def residual(r, st):
        h = st[0] + 0.5 * _rms(st[1], gpost_ref[...])
        h_ref[rows[r], :] = h
        return _rms(h, gmix_ref[...]).astype(BF16)

    def project(r, u2):
        qkv = jnp.dot(u2, win_ref[...], preferred_element_type=F32)
        qkv_ref[rows[r], :] = qkv.astype(BF16)
        v_band = qkv[:, qkv.shape[1] - W_CH:]
        for j in range(sub_blocks):
            vt_ref[r * sub_blocks + j] = v_band[j * LANES:(j + 1) * LANES, :].T.astype(BF16)

    _emit_merged(_Chain(ROW_SPLIT, [prenorm, ffn, residual, project], skewed=True).steps)


def _sub_tiles():
    sub = ROW_TILE // ROW_SPLIT
    return [slice(r * sub, (r + 1) * sub) for r in range(ROW_SPLIT)]


def _resident(shape):
    nd = len(shape)
    return pl.BlockSpec(shape, lambda *_: (0,) * nd, pipeline_mode=pl.Buffered(1))


def _cast_block_spec(shape, n_steps, first_step):
    rows, cols = shape
    blk = next(r for r in range(BF16_SUBLANES, rows + 1, BF16_SUBLANES)
               if rows % r == 0 and r * n_steps >= rows)
    last = rows // blk - 1
    return pl.BlockSpec((blk, cols), lambda i: (jnp.clip(i - first_step, 0, last), 0))


def _ffn_qkv(x2, gpre, gpost, wg, wu, wd, gmix, win, later_weights):
    n, d = x2.shape
    n_steps = n // ROW_TILE
    own = [wg, wu, wd, win]
    assert all(w.shape[0] % (WEIGHT_STEPS * BF16_SUBLANES) == 0 for w in own)
    tile_index = lambda i: jnp.maximum(i - WEIGHT_STEPS, 0)
    row = lambda w: pl.BlockSpec((ROW_TILE, w), lambda i: (tile_index(i), 0))
    vt_spec = pl.BlockSpec((ROW_TILE // LANES, W_CH, LANES), lambda i: (tile_index(i), 0, 0))
    own_specs = [pl.BlockSpec((w.shape[0] // WEIGHT_STEPS, w.shape[1]),
                              lambda i: (jnp.minimum(i, WEIGHT_STEPS - 1), 0)) for w in own]
    cast_specs = [_cast_block_spec(w.shape, n_steps, WEIGHT_STEPS) for w in later_weights]
    outs = pl.pallas_call(
        functools.partial(_ffn_qkv_kernel, len(later_weights)),
        grid=(WEIGHT_STEPS + n_steps,),
        in_specs=[row(d), _resident((1, d)), _resident((1, d)), *own_specs[:3],
                  _resident((1, d)), own_specs[3]] + cast_specs,
        out_specs=[row(d), row(win.shape[1]), vt_spec] + cast_specs,
        out_shape=[jax.ShapeDtypeStruct((n, d), F32),
                   jax.ShapeDtypeStruct((n, win.shape[1]), BF16),
                   jax.ShapeDtypeStruct((n // LANES, W_CH, LANES), BF16)]
                  + [jax.ShapeDtypeStruct(w.shape, BF16) for w in later_weights],
        scratch_shapes=[pltpu.VMEM(w.shape, BF16) for w in own],
        compiler_params=pltpu.CompilerParams(
            dimension_semantics=("arbitrary",), vmem_limit_bytes=DENSE_VMEM_BYTES),
        name="ffn1_qkv",
    )(x2, gpre, gpost, wg, wu, wd, gmix, win, *later_weights)
    return outs[0], outs[1], outs[2], outs[3:]


def _stack_heads(q2):
    lane = lax.broadcasted_iota(jnp.int32, q2.shape, 1)
    zero = jnp.zeros_like(q2)
    qs = q2 * jnp.asarray(SCALE, q2.dtype)
    return jnp.concatenate([jnp.where(lane < HEAD_DIM, qs, zero),
                            jnp.where(lane >= HEAD_DIM, qs, zero)], axis=0)


_NT = (((1,), (1,)), ((), ()))


class _Chain:
    def __init__(self, n_units, stages, skewed):
        self.state = [None] * n_units
        if skewed:
            order = [(s - k, k) for s in range(n_units + len(stages) - 1)
                     for k in reversed(range(len(stages))) if 0 <= s - k < n_units]
        else:
            order = [(u, k) for k in range(len(stages)) for u in range(n_units)]
        self.steps = [functools.partial(self._apply, stages[k], u) for u, k in order]

    def _apply(self, stage, u):
        self.state[u] = stage(u, self.state[u])


def _emit_merged(*step_lists):
    tagged = sorted(((i + 0.5) / len(steps), n, i)
                    for n, steps in enumerate(step_lists) for i in range(len(steps)))
    for _, n, i in tagged:
        step_lists[n][i]()


def _stickbreak_tile(q_ref, k_ref, v_ref, o_ref, carry_ref, qi, row0, other_steps):
    t = SB_TILE
    npair = W_SB // LANES
    pair_cols = [slice(hp * LANES, (hp + 1) * LANES) for hp in range(npair)]

    neg_after = jnp.where(lax.broadcasted_iota(jnp.int32, (t, t), 0)
                          > lax.broadcasted_iota(jnp.int32, (t, t), 1), -1.0, 0.0).astype(BF16)
    q_stacks = [_stack_heads(q_ref[0, row0:row0 + t, c]) for c in pair_cols]

    def sweep(blk, rows, causal=False, first=False, gate=None):
        r0, nr = rows
        start = pl.multiple_of(blk * t, t)
        heads_rows = (slice(r0, r0 + nr), slice(t + r0, t + r0 + nr))
        out_rows = slice(row0 + r0, row0 + r0 + nr)
        first_head = lax.broadcasted_iota(jnp.int32, (nr, LANES), 1) < HEAD_DIM
        if causal:
            ri = lax.broadcasted_iota(jnp.int32, (2 * nr, t), 0)
            ci = lax.broadcasted_iota(jnp.int32, (2 * nr, t), 1)
            before = ci < jnp.where(ri >= nr, ri - nr, ri) + r0

        def scores(hp, _):
            q = q_stacks[hp]
            if nr != t:
                q = jnp.concatenate([q[s] for s in heads_rows], axis=0)
            return lax.dot_general(q, k_ref[0, pl.ds(start, t), pair_cols[hp]], _NT,
                                   preferred_element_type=F32)

        def split(hp, z):
            sp = jnp.maximum(z, 0.0) + jnp.log(1.0 + jnp.exp(-jnp.abs(z)))
            if causal:
                sp = jnp.where(before, sp, 0.0)
            return z - sp, sp[:, 0:1], sp.astype(BF16)

        def cumsum(hp, st):
            return st[0], st[1], jnp.dot(st[2], neg_after, preferred_element_type=F32)

        def weights(hp, st):
            log_sig, sp_first, c = st
            p = jnp.exp(log_sig + c)
            if causal:
                p = jnp.where(before, p, 0.0)
            return c[:, 0:1] - sp_first, p.astype(BF16)

        def values(hp, st):
            return st[0], jnp.dot(st[1], v_ref[0, pl.ds(start, t), pair_cols[hp]],
                                  preferred_element_type=F32)

        def update(hp, st):
            tot, pv = st
            if first:
                new = tot
                o_ref[0, out_rows, pair_cols[hp]] = jnp.where(first_head, pv[:nr], pv[nr:])
            else:
                old = jnp.concatenate([carry_ref[hp, s] for s in heads_rows], axis=0)
                scale = jnp.exp(old) if gate is None else jnp.exp(old) * gate
                pv, new = scale * pv, old + tot
                o_ref[0, out_rows, pair_cols[hp]] += jnp.where(first_head, pv[:nr], pv[nr:])
            carry_ref[hp, heads_rows[0]] = new[:nr]
            carry_ref[hp, heads_rows[1]] = new[nr:]
            return new

        return _Chain(npair, [scores, split, cumsum, weights, values, update], skewed=True)

    def worst_carry(chain, rows, lo=None, hi=None):
        r0, nr = rows
        lo, hi = (r0 if lo is None else lo) - r0, (r0 + nr if hi is None else hi) - r0
        worst = None
        for c in chain.state:
            for part in (c[lo:hi], c[nr + lo:nr + hi]):
                worst = part if worst is None else jnp.maximum(worst, part)
        return jnp.max(worst)

    full, near, far = (0, t), (0, SB_NEAR_ROWS), (SB_NEAR_ROWS, t - SB_NEAR_ROWS)
    prev = jnp.maximum(qi - 1, 0)
    has_prev = qi > 0
    own = sweep(qi, full, causal=True, first=True)
    prev_near = sweep(prev, near, gate=has_prev.astype(F32))
    _emit_merged(own.steps + prev_near.steps, other_steps)

    worst_near = worst_carry(prev_near, near)
    worst_far = worst_carry(own, full, lo=SB_NEAR_ROWS)

    @pl.when(jnp.logical_and(has_prev, worst_far >= EXP_ZERO_BELOW))
    def _():
        _emit_merged(sweep(prev, far).steps)

    def cond(st):
        return jnp.logical_and(st[0] < qi, st[1] >= EXP_ZERO_BELOW)

    def body(st):
        chain = sweep(qi - 1 - st[0], full)
        _emit_merged(chain.steps)
        return st[0] + 1, worst_carry(chain, full)

    lax.while_loop(cond, body, (jnp.int32(1), jnp.maximum(worst_near, worst_far)))


def _bias_by_offset(rel_bias):
    h = rel_bias.shape[0]
    rb = rel_bias.astype(F32)
    far = LOOKBACK * CHUNK + CH_TQ - 1 - REL_CLIP
    near = CH_EXT - far - (2 * REL_CLIP + 1)
    ext = jnp.concatenate([jnp.broadcast_to(rb[:, -1:], (h, far)), rb[:, ::-1],
                           jnp.broadcast_to(rb[:, :1], (h, near))], axis=1)
    return ext.reshape(h, 1, CH_EXT)


def _build_bias_table(ext_ref, tab_ref):
    shape = (CH_TQ, CH_EXT)
    i = lax.broadcasted_iota(jnp.int32, shape, 0)
    j = lax.broadcasted_iota(jnp.int32, shape, 1)
    dchunk = j // CHUNK - i // CHUNK
    valid = (dchunk >= 0) & (dchunk <= LOOKBACK) & (j < CH_WIN)
    for h in range(H_CH):
        rolled = pltpu.roll(jnp.broadcast_to(ext_ref[h], shape), CH_EXT - (CH_TQ - 1), 1,
                            stride=1, stride_axis=0)
        tab = jnp.where(valid, rolled, NEG_INF)
        for o in range(CH_TAB_BLKS):
            tab_ref[h, o] = tab[:, o * LANES:(o + 1) * LANES].T


def _band_tiles(q_ref, k_ref, vt_ref, o_ref, tab_ref, qi, row0):
    npair = W_CH // LANES
    units = [(sub, hp) for sub in range(CH_SUB) for hp in range(npair)]
    rows = [slice(row0 + sub * CH_TQ, row0 + (sub + 1) * CH_TQ) for sub in range(CH_SUB)]
    cols = [slice(hp * LANES, (hp + 1) * LANES) for hp in range(npair)]
    starts, shifts = [], []
    for sub in range(CH_SUB):
        nominal = (qi * CH_SUB + sub) * CH_TQ - LOOKBACK * CHUNK
        start = pl.multiple_of(jnp.maximum(nominal, 0), LANES)
        starts.append(start)
        shifts.append((start - nominal) // LANES)

    def scores(u, _):
        sub, hp = units[u]
        return lax.dot_general(k_ref[0, pl.ds(starts[sub], CH_WIN), cols[hp]],
                               _stack_heads(q_ref[0, rows[sub], cols[hp]]),
                               _NT, preferred_element_type=F32)

    def softmax(u, z):
        sub, hp = units[u]
        bias = jnp.concatenate(
            [jnp.concatenate([tab_ref[2 * hp + h, shifts[sub] + o] for h in range(2)], axis=-1)
             for o in range(CH_NBLK)], axis=0)
        z = z + bias
        m = jnp.max(z, axis=0, keepdims=True)
        p = jnp.exp(z - m)
        return jnp.sum(p, axis=0, keepdims=True), p.astype(BF16)

    def values(u, st):
        sub, hp = units[u]
        first_blk = starts[sub] // LANES
        vt = jnp.concatenate([vt_ref[0, first_blk + o, cols[hp], :] for o in range(CH_NBLK)],
                             axis=-1)
        return st[0], jnp.dot(vt, st[1], preferred_element_type=F32)

    def store(u, st):
        sub, hp = units[u]
        out = st[1] / st[0]
        own = jnp.concatenate([out[0:HEAD_DIM, 0:CH_TQ], out[HEAD_DIM:, CH_TQ:]], axis=0)
        o_ref[0, rows[sub], cols[hp]] = own.T

    return _Chain(len(units), [scores, softmax, values, store], skewed=False).steps


def _attn_kernel(ext_ref, qa_ref, ka_ref, va_ref, qb_ref, kb_ref, vb_ref, oa_ref, ob_ref,
                 carry_ref, tab_ref):
    @pl.when((pl.program_id(0) == 0) & (pl.program_id(1) == 0))
    def _():
        _build_bias_table(ext_ref, tab_ref)

    for tile in range(MIX_TILES):
        qi = pl.program_id(1) * MIX_TILES + tile
        row0 = tile * SB_TILE
        _stickbreak_tile(qa_ref, ka_ref, va_ref, oa_ref, carry_ref, qi, row0,
                         _band_tiles(qb_ref, kb_ref, vb_ref, ob_ref, tab_ref, qi, row0))


def _attention(qkv, vt_band, ext, b, s):
    assert W_SB == W_CH and SB_TILE == CH_SUB * CH_TQ
    t, w = MIX_TILES * SB_TILE, W_SB
    tile = lambda c: pl.BlockSpec((1, t, w), lambda bi, qi: (bi, qi, c))
    whole = lambda c: pl.BlockSpec((1, s, w), lambda bi, qi: (bi, 0, c))
    vt_spec = pl.BlockSpec((1,) + vt_band.shape[1:], lambda bi, qi: (bi, 0, 0, 0))
    return pl.pallas_call(
        _attn_kernel,
        grid=(b, s // t),
        in_specs=[_resident(ext.shape), tile(0), whole(1), whole(2), tile(3), whole(4), vt_spec],
        out_specs=[tile(0), tile(0)],
        out_shape=[jax.ShapeDtypeStruct((b, s, w), F32)] * 2,
        scratch_shapes=[pltpu.VMEM((w // LANES, 2 * SB_TILE, 1), F32),
                        pltpu.VMEM((H_CH, CH_TAB_BLKS, LANES, CH_TQ), F32)],
        compiler_params=pltpu.CompilerParams(
            dimension_semantics=("arbitrary", "arbitrary"),
            vmem_limit_bytes=ATTN_VMEM_BYTES),
        name="mixers",
    )(ext, *([qkv] * 5), vt_band)


def _tail_kernel(h_ref, oa_ref, ob_ref, p_ref, gsb_ref, gch_ref, wout_ref, gmixpost_ref,
                 gpre_ref, gpost_ref, wg_ref, wu_ref, wd_ref, wple_ref, wgate_ref, gple_ref,
                 out_ref):
    rows = _sub_tiles()

    def mix_norm(r, _):
        return jnp.concatenate([_rms(oa_ref[rows[r], :], gsb_ref[...]).astype(BF16),
                                _rms(ob_ref[rows[r], :], gch_ref[...]).astype(BF16)], axis=-1)

    def out_proj(r, mixed):
        y = jnp.dot(mixed, wout_ref[...], preferred_element_type=F32)
        proj = jnp.dot(p_ref[rows[r], :].astype(BF16), wple_ref[...], preferred_element_type=F32)
        return y, proj

    def mix_residual(r, st):
        h = h_ref[rows[r], :] + _rms(st[0], gmixpost_ref[...])
        return h, st[1], _rms(h, gpre_ref[...]).astype(BF16)

    def ffn(r, st):
        return st[0], st[1], _swiglu(st[2], wg_ref, wu_ref, wd_ref)

    def ffn_residual(r, st):
        h = st[0] + 0.5 * _rms(st[2], gpost_ref[...])
        return h, st[1], h.astype(BF16)

    def ple_gate(r, st):
        return st[0], st[1], jnp.dot(st[2], wgate_ref[...], preferred_element_type=F32)

    def ple_residual(r, st):
        h, proj, gate = st
        e = proj * jax.nn.sigmoid(gate)
        out_ref[rows[r], :] = h + _rms(e, gple_ref[...])

    _emit_merged(_Chain(ROW_SPLIT, [mix_norm, out_proj, mix_residual, ffn, ffn_residual,
                                    ple_gate, ple_residual], skewed=True).steps)


def _tail(h, oa, ob, p2, gsb, gch, wout, gmixpost, gpre, gpost, wg, wu, wd, wple, wgate, gple):
    n, d = h.shape
    d_ff = wg.shape[1]
    row = lambda w: pl.BlockSpec((ROW_TILE, w), lambda i: (i, 0))
    return pl.pallas_call(
        _tail_kernel,
        grid=(n // ROW_TILE,),
        in_specs=[row(d), row(oa.shape[1]), row(ob.shape[1]), row(p2.shape[1]),
                  _resident((1, oa.shape[1])), _resident((1, ob.shape[1])),
                  _resident(wout.shape), _resident((1, d)),
                  _resident((1, d)), _resident((1, d)),
                  _resident((d, d_ff)), _resident((d, d_ff)), _resident((d_ff, d)),
                  _resident(wple.shape), _resident(wgate.shape), _resident((1, d))],
        out_specs=row(d),
        out_shape=jax.ShapeDtypeStruct((n, d), F32),
        compiler_params=pltpu.CompilerParams(
            dimension_semantics=("arbitrary",), vmem_limit_bytes=DENSE_VMEM_BYTES),
        name="outproj_ffn2_ple",
    )(h, oa, ob, p2, gsb, gch, wout, gmixpost, gpre, gpost, wg, wu, wd, wple, wgate, gple)


def kernel(x, p, g_ffn1_pre, g_ffn1_post, w_ffn1_gate, w_ffn1_up, w_ffn1_down, g_mix_pre, g_mix_post, w_in, g_out_sb, g_out_ch, rel_bias, w_out, g_ffn2_pre, g_ffn2_post, w_ffn2_gate, w_ffn2_up, w_ffn2_down, w_ple_proj, w_ple_gate, g_ple_post):
    b, s, d = x.shape
    depth = p.shape[0]
    n = b * s
    gain = lambda g: g.astype(F32).reshape(1, -1)
    h = x.reshape(n, d)
    for i in range(depth):
        later = [w_out[i], w_ffn2_gate[i], w_ffn2_up[i], w_ffn2_down[i],
                 w_ple_proj[i], w_ple_gate[i]]
        h1, qkv, vt_band, later_bf16 = _ffn_qkv(
            h, gain(g_ffn1_pre[i]), gain(g_ffn1_post[i]),
            w_ffn1_gate[i], w_ffn1_up[i], w_ffn1_down[i], gain(g_mix_pre[i]), w_in[i], later)
        wout, wg2, wu2, wd2, wple, wgate = later_bf16
        qkv3 = qkv.reshape(b, s, qkv.shape[1])
        vt_band = vt_band.reshape(b, s // LANES, W_CH, LANES)
        o_a, o_b = _attention(qkv3, vt_band, _bias_by_offset(rel_bias[i]), b, s)
        h = _tail(h1, o_a.reshape(n, W_SB), o_b.reshape(n, W_CH), p[i].reshape(n, -1),
                  gain(g_out_sb[i]), gain(g_out_ch[i]), wout, gain(g_mix_post[i]),
                  gain(g_ffn2_pre[i]), gain(g_ffn2_post[i]), wg2, wu2, wd2,
                  wple, wgate, gain(g_ple_post[i]))
    return h.reshape(b, s, d)
```

```python
import functools

import jax
import jax.numpy as jnp
from jax import lax
from jax.experimental import pallas as pl
from jax.experimental.pallas import tpu as pltpu

HEAD_DIM = 64
H_SB = 8
H_CH = 8
W_SB = H_SB * HEAD_DIM
W_CH = H_CH * HEAD_DIM
CHUNK = 64
LOOKBACK = 8
REL_CLIP = 128
EPS = 1e-6
NEG_INF = -1e30
SCALE = HEAD_DIM ** -0.5

LANES = 128
BF16_SUBLANES = 16
MXU_DIM = 256
ROW_TILE = 512
ROW_SPLIT = 2
WEIGHT_STEPS = 16
FF_SPLIT = 2
SB_TILE = 256
SB_NEAR_ROWS = 160
CH_TQ = 128
CH_WIN = LOOKBACK * CHUNK + CH_TQ
CH_NBLK = CH_WIN // LANES
CH_TAB_BLKS = CH_NBLK + LOOKBACK * CHUNK // LANES
CH_EXT = (CH_TAB_BLKS + 1) * LANES
CH_SUB = 2
EXP_ZERO_BELOW = -104.0
DENSE_VMEM_BYTES = 58 * 1024 * 1024
ATTN_VMEM_BYTES = 48 * 1024 * 1024

F32 = jnp.float32
BF16 = jnp.bfloat16


def _rms(x, g):
    ms = jnp.mean(x * x, axis=-1, keepdims=True)
    return x * lax.rsqrt(ms + EPS) * g


def _swiglu(u, wg_ref, wu_ref, wd_ref):
    d_ff = wg_ref.shape[1]
    tiles = d_ff // MXU_DIM
    edges = [((c * tiles + FF_SPLIT - 1) // FF_SPLIT) * MXU_DIM for c in range(FF_SPLIT)] + [d_ff]
    f = None
    for lo, hi in zip(edges[:-1], edges[1:]):
        gate = jnp.dot(u, wg_ref[:, lo:hi], preferred_element_type=F32)
        up = jnp.dot(u, wu_ref[:, lo:hi], preferred_element_type=F32)
        act = (gate * jax.nn.sigmoid(gate) * up).astype(BF16)
        part = jnp.dot(act, wd_ref[lo:hi, :], preferred_element_type=F32)
        f = part if f is None else f + part
    return f


def _ffn_qkv_kernel(n_cast, x_ref, gpre_ref, gpost_ref, wg_blk, wu_blk, wd_blk,
                    gmix_ref, win_blk, *rest):
    cast_in, (h_ref, qkv_ref, vt_ref) = rest[:n_cast], rest[n_cast:n_cast + 3]
    cast_out = rest[n_cast + 3:2 * n_cast + 3]
    wg_ref, wu_ref, wd_ref, win_ref = rest[2 * n_cast + 3:]
    step = pl.program_id(0)

    @pl.when(step < WEIGHT_STEPS)
    def _():
        for blk, dst in ((wg_blk, wg_ref), (wu_blk, wu_ref), (wd_blk, wd_ref), (win_blk, win_ref)):
            nr = blk.shape[0]
            dst[pl.ds(pl.multiple_of(step * nr, nr), nr), :] = blk[...].astype(BF16)

    @pl.when(step >= WEIGHT_STEPS)
    def _():
        _ffn_qkv_tile(x_ref, gpre_ref, gpost_ref, wg_ref, wu_ref, wd_ref, gmix_ref, win_ref,
                      cast_in, h_ref, qkv_ref, vt_ref, cast_out)


def _ffn_qkv_tile(x_ref, gpre_ref, gpost_ref, wg_ref, wu_ref, wd_ref, gmix_ref, win_ref,
                  cast_in, h_ref, qkv_ref, vt_ref, cast_out):
    for src, dst in zip(cast_in, cast_out):
        dst[...] = src[...].astype(BF16)
    rows = _sub_tiles()
    sub_blocks = ROW_TILE // ROW_SPLIT // LANES

    def prenorm(r, _):
        x = x_ref[rows[r], :]
        return x, _rms(x, gpre_ref[...]).astype(BF16)

    def ffn(r, st):
        return st[0], _swiglu(st[1], wg_ref, wu_ref, wd_ref)

    def residual(r, st):
        h = st[0] + 0.5 * _rms(st[1], gpost_ref[...])
        h_ref[rows[r], :] = h
        return _rms(h, gmix_ref[...]).astype(BF16)

    def project(r, u2):
        qkv = jnp.dot(u2, win_ref[...], preferred_element_type=F32)
        qkv_ref[rows[r], :] = qkv.astype(BF16)
        v_band = qkv[:, qkv.shape[1] - W_CH:]
        for j in range(sub_blocks):
            vt_ref[r * sub_blocks + j] = v_band[j * LANES:(j + 1) * LANES, :].T.astype(BF16)

    _emit_merged(_Chain(ROW_SPLIT, [prenorm, ffn, residual, project], skewed=True).steps)


def _sub_tiles():
    sub = ROW_TILE // ROW_SPLIT
    return [slice(r * sub, (r + 1) * sub) for r in range(ROW_SPLIT)]


def _resident(shape):
    nd = len(shape)
    return pl.BlockSpec(shape, lambda *_: (0,) * nd, pipeline_mode=pl.Buffered(1))


def _cast_block_spec(shape, n_steps, first_step):
    rows, cols = shape
    blk = next(r for r in range(BF16_SUBLANES, rows + 1, BF16_SUBLANES)
               if rows % r == 0 and r * n_steps >= rows)
    last = rows // blk - 1
    return pl.BlockSpec((blk, cols), lambda i: (jnp.clip(i - first_step, 0, last), 0))


def _ffn_qkv(x2, gpre, gpost, wg, wu, wd, gmix, win, later_weights):
    n, d = x2.shape
    n_steps = n // ROW_TILE
    own = [wg, wu, wd, win]
    assert all(w.shape[0] % (WEIGHT_STEPS * BF16_SUBLANES) == 0 for w in own)
    tile_index = lambda i: jnp.maximum(i - WEIGHT_STEPS, 0)
    row = lambda w: pl.BlockSpec((ROW_TILE, w), lambda i: (tile_index(i), 0))
    vt_spec = pl.BlockSpec((ROW_TILE // LANES, W_CH, LANES), lambda i: (tile_index(i), 0, 0))
    own_specs = [pl.BlockSpec((w.shape[0] // WEIGHT_STEPS, w.shape[1]),
                              lambda i: (jnp.minimum(i, WEIGHT_STEPS - 1), 0)) for w in own]
    cast_specs = [_cast_block_spec(w.shape, n_steps, WEIGHT_STEPS) for w in later_weights]
    outs = pl.pallas_call(
        functools.partial(_ffn_qkv_kernel, len(later_weights)),
        grid=(WEIGHT_STEPS + n_steps,),
        in_specs=[row(d), _resident((1, d)), _resident((1, d)), *own_specs[:3],
                  _resident((1, d)), own_specs[3]] + cast_specs,
        out_specs=[row(d), row(win.shape[1]), vt_spec] + cast_specs,
        out_shape=[jax.ShapeDtypeStruct((n, d), F32),
                   jax.ShapeDtypeStruct((n, win.shape[1]), BF16),
                   jax.ShapeDtypeStruct((n // LANES, W_CH, LANES), BF16)]
                  + [jax.ShapeDtypeStruct(w.shape, BF16) for w in later_weights],
        scratch_shapes=[pltpu.VMEM(w.shape, BF16) for w in own],
        compiler_params=pltpu.CompilerParams(
            dimension_semantics=("arbitrary",), vmem_limit_bytes=DENSE_VMEM_BYTES),
        name="ffn1_qkv",
    )(x2, gpre, gpost, wg, wu, wd, gmix, win, *later_weights)
    return outs[0], outs[1], outs[2], outs[3:]


def _stack_heads(q2):
    lane = lax.broadcasted_iota(jnp.int32, q2.shape, 1)
    zero = jnp.zeros_like(q2)
    qs = q2 * jnp.asarray(SCALE, q2.dtype)
    return jnp.concatenate([jnp.where(lane < HEAD_DIM, qs, zero),
                            jnp.where(lane >= HEAD_DIM, qs, zero)], axis=0)


_NT = (((1,), (1,)), ((), ()))


class _Chain:
    def __init__(self, n_units, stages, skewed):
        self.state = [None] * n_units
        if skewed:
            order = [(s - k, k) for s in range(n_units + len(stages) - 1)
                     for k in range(len(stages)) if 0 <= s - k < n_units]
        else:
            order = [(u, k) for k in range(len(stages)) for u in range(n_units)]
        self.steps = [functools.partial(self._apply, stages[k], u) for u, k in order]

    def _apply(self, stage, u):
        self.state[u] = stage(u, self.state[u])


def _emit_merged(*step_lists):
    tagged = sorted(((i + 0.5) / len(steps), n, i)
                    for n, steps in enumerate(step_lists) for i in range(len(steps)))
    for _, n, i in tagged:
        step_lists[n][i]()


def _stickbreak_tile(q_ref, k_ref, v_ref, o_ref, carry_ref, qi, other_steps):
    t = SB_TILE
    npair = W_SB // LANES
    pair_cols = [slice(hp * LANES, (hp + 1) * LANES) for hp in range(npair)]

    neg_after = jnp.where(lax.broadcasted_iota(jnp.int32, (t, t), 0)
                          > lax.broadcasted_iota(jnp.int32, (t, t), 1), -1.0, 0.0).astype(BF16)
    q_stacks = [_stack_heads(q_ref[0, :, c]) for c in pair_cols]

    def sweep(blk, rows, causal=False, first=False, gate=None):
        r0, nr = rows
        start = pl.multiple_of(blk * t, t)
        heads_rows = (slice(r0, r0 + nr), slice(t + r0, t + r0 + nr))
        first_head = lax.broadcasted_iota(jnp.int32, (nr, LANES), 1) < HEAD_DIM
        if causal:
            ri = lax.broadcasted_iota(jnp.int32, (2 * nr, t), 0)
            ci = lax.broadcasted_iota(jnp.int32, (2 * nr, t), 1)
            before = ci < jnp.where(ri >= nr, ri - nr, ri) + r0

        def scores(hp, _):
            q = q_stacks[hp]
            if nr != t:
                q = jnp.concatenate([q[s] for s in heads_rows], axis=0)
            return lax.dot_general(q, k_ref[0, pl.ds(start, t), pair_cols[hp]], _NT,
                                   preferred_element_type=F32)

        def split(hp, z):
            sp = jnp.maximum(z, 0.0) + jnp.log(1.0 + jnp.exp(-jnp.abs(z)))
            if causal:
                sp = jnp.where(before, sp, 0.0)
            return z - sp, sp[:, 0:1], sp.astype(BF16)

        def cumsum(hp, st):
            return st[0], st[1], jnp.dot(st[2], neg_after, preferred_element_type=F32)

        def weights(hp, st):
            log_sig, sp_first, c = st
            p = jnp.exp(log_sig + c)
            if causal:
                p = jnp.where(before, p, 0.0)
            return c[:, 0:1] - sp_first, p.astype(BF16)

        def values(hp, st):
            return st[0], jnp.dot(st[1], v_ref[0, pl.ds(start, t), pair_cols[hp]],
                                  preferred_element_type=F32)

        def update(hp, st):
            tot, pv = st
            if first:
                new = tot
                o_ref[0, heads_rows[0], pair_cols[hp]] = jnp.where(first_head, pv[:nr], pv[nr:])
            else:
                old = jnp.concatenate([carry_ref[hp, s] for s in heads_rows], axis=0)
                scale = jnp.exp(old) if gate is None else jnp.exp(old) * gate
                pv, new = scale * pv, old + tot
                o_ref[0, heads_rows[0], pair_cols[hp]] += jnp.where(first_head, pv[:nr], pv[nr:])
            carry_ref[hp, heads_rows[0]] = new[:nr]
            carry_ref[hp, heads_rows[1]] = new[nr:]
            return new

        return _Chain(npair, [scores, split, cumsum, weights, values, update], skewed=True)

    def worst_carry(chain, rows, lo=None, hi=None):
        r0, nr = rows
        lo, hi = (r0 if lo is None else lo) - r0, (r0 + nr if hi is None else hi) - r0
        worst = None
        for c in chain.state:
            for part in (c[lo:hi], c[nr + lo:nr + hi]):
                worst = part if worst is None else jnp.maximum(worst, part)
        return jnp.max(worst)

    full, near, far = (0, t), (0, SB_NEAR_ROWS), (SB_NEAR_ROWS, t - SB_NEAR_ROWS)
    prev = jnp.maximum(qi - 1, 0)
    has_prev = qi > 0
    own = sweep(qi, full, causal=True, first=True)
    prev_near = sweep(prev, near, gate=has_prev.astype(F32))
    _emit_merged(own.steps + prev_near.steps, other_steps)

    worst_near = worst_carry(prev_near, near)
    worst_far = worst_carry(own, full, lo=SB_NEAR_ROWS)

    @pl.when(jnp.logical_and(has_prev, worst_far >= EXP_ZERO_BELOW))
    def _():
        _emit_merged(sweep(prev, far).steps)

    def cond(st):
        return jnp.logical_and(st[0] < qi, st[1] >= EXP_ZERO_BELOW)

    def body(st):
        chain = sweep(qi - 1 - st[0], full)
        _emit_merged(chain.steps)
        return st[0] + 1, worst_carry(chain, full)

    lax.while_loop(cond, body, (jnp.int32(1), jnp.maximum(worst_near, worst_far)))


def _bias_by_offset(rel_bias):
    h = rel_bias.shape[0]
    rb = rel_bias.astype(F32)
    far = LOOKBACK * CHUNK + CH_TQ - 1 - REL_CLIP
    near = CH_EXT - far - (2 * REL_CLIP + 1)
    ext = jnp.concatenate([jnp.broadcast_to(rb[:, -1:], (h, far)), rb[:, ::-1],
                           jnp.broadcast_to(rb[:, :1], (h, near))], axis=1)
    return ext.reshape(h, 1, CH_EXT)


def _build_bias_table(ext_ref, tab_ref):
    shape = (CH_TQ, CH_EXT)
    i = lax.broadcasted_iota(jnp.int32, shape, 0)
    j = lax.broadcasted_iota(jnp.int32, shape, 1)
    dchunk = j // CHUNK - i // CHUNK
    valid = (dchunk >= 0) & (dchunk <= LOOKBACK) & (j < CH_WIN)
    for h in range(H_CH):
        rolled = pltpu.roll(jnp.broadcast_to(ext_ref[h], shape), CH_EXT - (CH_TQ - 1), 1,
                            stride=1, stride_axis=0)
        tab = jnp.where(valid, rolled, NEG_INF)
        for o in range(CH_TAB_BLKS):
            tab_ref[h, o] = tab[:, o * LANES:(o + 1) * LANES].T


def _band_tiles(q_ref, k_ref, vt_ref, o_ref, tab_ref, qi):
    npair = W_CH // LANES
    units = [(sub, hp) for sub in range(CH_SUB) for hp in range(npair)]
    rows = [slice(sub * CH_TQ, (sub + 1) * CH_TQ) for sub in range(CH_SUB)]
    cols = [slice(hp * LANES, (hp + 1) * LANES) for hp in range(npair)]
    starts, shifts = [], []
    for sub in range(CH_SUB):
        nominal = (qi * CH_SUB + sub) * CH_TQ - LOOKBACK * CHUNK
        start = pl.multiple_of(jnp.maximum(nominal, 0), LANES)
        starts.append(start)
        shifts.append((start - nominal) // LANES)

    def scores(u, _):
        sub, hp = units[u]
        return lax.dot_general(k_ref[0, pl.ds(starts[sub], CH_WIN), cols[hp]],
                               _stack_heads(q_ref[0, rows[sub], cols[hp]]),
                               _NT, preferred_element_type=F32)

    def softmax(u, z):
        sub, hp = units[u]
        bias = jnp.concatenate(
            [jnp.concatenate([tab_ref[2 * hp + h, shifts[sub] + o] for h in range(2)], axis=-1)
             for o in range(CH_NBLK)], axis=0)
        z = z + bias
        m = jnp.max(z, axis=0, keepdims=True)
        p = jnp.exp(z - m)
        return jnp.sum(p, axis=0, keepdims=True), p.astype(BF16)

    def values(u, st):
        sub, hp = units[u]
        first_blk = starts[sub] // LANES
        vt = jnp.concatenate([vt_ref[0, first_blk + o, cols[hp], :] for o in range(CH_NBLK)],
                             axis=-1)
        return st[0], jnp.dot(vt, st[1], preferred_element_type=F32)

    def store(u, st):
        sub, hp = units[u]
        out = st[1] / st[0]
        own = jnp.concatenate([out[0:HEAD_DIM, 0:CH_TQ], out[HEAD_DIM:, CH_TQ:]], axis=0)
        o_ref[0, rows[sub], cols[hp]] = own.T

    return _Chain(len(units), [scores, softmax, values, store], skewed=False).steps


def _attn_kernel(ext_ref, qa_ref, ka_ref, va_ref, qb_ref, kb_ref, vb_ref, oa_ref, ob_ref,
                 carry_ref, tab_ref):
    @pl.when((pl.program_id(0) == 0) & (pl.program_id(1) == 0))
    def _():
        _build_bias_table(ext_ref, tab_ref)

    qi = pl.program_id(1)
    _stickbreak_tile(qa_ref, ka_ref, va_ref, oa_ref, carry_ref, qi,
                     _band_tiles(qb_ref, kb_ref, vb_ref, ob_ref, tab_ref, qi))


def _attention(qkv, vt_band, ext, b, s):
    assert W_SB == W_CH and SB_TILE == CH_SUB * CH_TQ
    t, w = SB_TILE, W_SB
    tile = lambda c: pl.BlockSpec((1, t, w), lambda bi, qi: (bi, qi, c))
    whole = lambda c: pl.BlockSpec((1, s, w), lambda bi, qi: (bi, 0, c))
    vt_spec = pl.BlockSpec((1,) + vt_band.shape[1:], lambda bi, qi: (bi, 0, 0, 0))
    return pl.pallas_call(
        _attn_kernel,
        grid=(b, s // t),
        in_specs=[_resident(ext.shape), tile(0), whole(1), whole(2), tile(3), whole(4), vt_spec],
        out_specs=[tile(0), tile(0)],
        out_shape=[jax.ShapeDtypeStruct((b, s, w), F32)] * 2,
        scratch_shapes=[pltpu.VMEM((w // LANES, 2 * t, 1), F32),
                        pltpu.VMEM((H_CH, CH_TAB_BLKS, LANES, CH_TQ), F32)],
        compiler_params=pltpu.CompilerParams(
            dimension_semantics=("arbitrary", "arbitrary"),
            vmem_limit_bytes=ATTN_VMEM_BYTES),
        name="mixers",
    )(ext, *([qkv] * 5), vt_band)


def _tail_kernel(h_ref, oa_ref, ob_ref, p_ref, gsb_ref, gch_ref, wout_ref, gmixpost_ref,
                 gpre_ref, gpost_ref, wg_ref, wu_ref, wd_ref, wple_ref, wgate_ref, gple_ref,
                 out_ref):
    rows = _sub_tiles()

    def mix_norm(r, _):
        return jnp.concatenate([_rms(oa_ref[rows[r], :], gsb_ref[...]).astype(BF16),
                                _rms(ob_ref[rows[r], :], gch_ref[...]).astype(BF16)], axis=-1)

    def out_proj(r, mixed):
        y = jnp.dot(mixed, wout_ref[...], preferred_element_type=F32)
        proj = jnp.dot(p_ref[rows[r], :].astype(BF16), wple_ref[...], preferred_element_type=F32)
        return y, proj

    def mix_residual(r, st):
        h = h_ref[rows[r], :] + _rms(st[0], gmixpost_ref[...])
        return h, st[1], _rms(h, gpre_ref[...]).astype(BF16)

    def ffn(r, st):
        return st[0], st[1], _swiglu(st[2], wg_ref, wu_ref, wd_ref)

    def ffn_residual(r, st):
        h = st[0] + 0.5 * _rms(st[2], gpost_ref[...])
        return h, st[1], h.astype(BF16)

    def ple_gate(r, st):
        return st[0], st[1], jnp.dot(st[2], wgate_ref[...], preferred_element_type=F32)

    def ple_residual(r, st):
        h, proj, gate = st
        e = proj * jax.nn.sigmoid(gate)
        out_ref[rows[r], :] = h + _rms(e, gple_ref[...])

    _emit_merged(_Chain(ROW_SPLIT, [mix_norm, out_proj, mix_residual, ffn, ffn_residual,
                                    ple_gate, ple_residual], skewed=True).steps)


def _tail(h, oa, ob, p2, gsb, gch, wout, gmixpost, gpre, gpost, wg, wu, wd, wple, wgate, gple):
    n, d = h.shape
    d_ff = wg.shape[1]
    row = lambda w: pl.BlockSpec((ROW_TILE, w), lambda i: (i, 0))
    return pl.pallas_call(
        _tail_kernel,
        grid=(n // ROW_TILE,),
        in_specs=[row(d), row(oa.shape[1]), row(ob.shape[1]), row(p2.shape[1]),
                  _resident((1, oa.shape[1])), _resident((1, ob.shape[1])),
                  _resident(wout.shape), _resident((1, d)),
                  _resident((1, d)), _resident((1, d)),
                  _resident((d, d_ff)), _resident((d, d_ff)), _resident((d_ff, d)),
                  _resident(wple.shape), _resident(wgate.shape), _resident((1, d))],
        out_specs=row(d),
        out_shape=jax.ShapeDtypeStruct((n, d), F32),
        compiler_params=pltpu.CompilerParams(
            dimension_semantics=("arbitrary",), vmem_limit_bytes=DENSE_VMEM_BYTES),
        name="outproj_ffn2_ple",
    )(h, oa, ob, p2, gsb, gch, wout, gmixpost, gpre, gpost, wg, wu, wd, wple, wgate, gple)


def kernel(x, p, g_ffn1_pre, g_ffn1_post, w_ffn1_gate, w_ffn1_up, w_ffn1_down, g_mix_pre, g_mix_post, w_in, g_out_sb, g_out_ch, rel_bias, w_out, g_ffn2_pre, g_ffn2_post, w_ffn2_gate, w_ffn2_up, w_ffn2_down, w_ple_proj, w_ple_gate, g_ple_post):
    b, s, d = x.shape
    depth = p.shape[0]
    n = b * s
    gain = lambda g: g.astype(F32).reshape(1, -1)
    h = x.reshape(n, d)
    for i in range(depth):
        later = [w_out[i], w_ffn2_gate[i], w_ffn2_up[i], w_ffn2_down[i],
                 w_ple_proj[i], w_ple_gate[i]]
        h1, qkv, vt_band, later_bf16 = _ffn_qkv(
            h, gain(g_ffn1_pre[i]), gain(g_ffn1_post[i]),
            w_ffn1_gate[i], w_ffn1_up[i], w_ffn1_down[i], gain(g_mix_pre[i]), w_in[i], later)
        wout, wg2, wu2, wd2, wple, wgate = later_bf16
        qkv3 = qkv.reshape(b, s, qkv.shape[1])
        vt_band = vt_band.reshape(b, s // LANES, W_CH, LANES)
        o_a, o_b = _attention(qkv3, vt_band, _bias_by_offset(rel_bias[i]), b, s)
        h = _tail(h1, o_a.reshape(n, W_SB), o_b.reshape(n, W_CH), p[i].reshape(n, -1),
                  gain(g_out_sb[i]), gain(g_out_ch[i]), wout, gain(g_mix_post[i]),
                  gain(g_ffn2_pre[i]), gain(g_ffn2_post[i]), wg2, wu2, wd2,
                  wple, wgate, gain(g_ple_post[i]))
    return h.reshape(b, s, d)
```

```python
import functools

import jax
import jax.numpy as jnp
from jax import lax
from jax.experimental import pallas as pl
from jax.experimental.pallas import tpu as pltpu

HEAD_DIM = 64
H_SB = 8
H_CH = 8
W_SB = H_SB * HEAD_DIM
W_CH = H_CH * HEAD_DIM
CHUNK = 64
LOOKBACK = 8
REL_CLIP = 128
EPS = 1e-6
NEG_INF = -1e30
SCALE = HEAD_DIM ** -0.5

LANES = 128
BF16_SUBLANES = 16
MXU_DIM = 256
ROW_TILE = 512
ROW_SPLIT = 2
WEIGHT_STEPS = 16
FF_SPLIT = 2
SB_TILE = 256
SB_NEAR_ROWS = 160
CH_TQ = 128
CH_WIN = LOOKBACK * CHUNK + CH_TQ
CH_NBLK = CH_WIN // LANES
CH_TAB_BLKS = CH_NBLK + LOOKBACK * CHUNK // LANES
CH_EXT = (CH_TAB_BLKS + 1) * LANES
CH_SUB = 2
EXP_ZERO_BELOW = -104.0
DENSE_VMEM_BYTES = 58 * 1024 * 1024
ATTN_VMEM_BYTES = 48 * 1024 * 1024

F32 = jnp.float32
BF16 = jnp.bfloat16


def _rms(x, g):
    ms = jnp.mean(x * x, axis=-1, keepdims=True)
    return x * lax.rsqrt(ms + EPS) * g


def _swiglu(u, wg_ref, wu_ref, wd_ref):
    d_ff = wg_ref.shape[1]
    tiles = d_ff // MXU_DIM
    edges = [((c * tiles + FF_SPLIT - 1) // FF_SPLIT) * MXU_DIM for c in range(FF_SPLIT)] + [d_ff]
    f = None
    for lo, hi in zip(edges[:-1], edges[1:]):
        gate = jnp.dot(u, wg_ref[:, lo:hi], preferred_element_type=F32)
        up = jnp.dot(u, wu_ref[:, lo:hi], preferred_element_type=F32)
        act = (gate * jax.nn.sigmoid(gate) * up).astype(BF16)
        part = jnp.dot(act, wd_ref[lo:hi, :], preferred_element_type=F32)
        f = part if f is None else f + part
    return f


def _ffn_qkv_kernel(n_cast, x_ref, gpre_ref, gpost_ref, wg_blk, wu_blk, wd_blk,
                    gmix_ref, win_blk, *rest):
    cast_in, (h_ref, qkv_ref, vt_ref) = rest[:n_cast], rest[n_cast:n_cast + 3]
    cast_out = rest[n_cast + 3:2 * n_cast + 3]
    wg_ref, wu_ref, wd_ref, win_ref = rest[2 * n_cast + 3:]
    step = pl.program_id(0)

    @pl.when(step < WEIGHT_STEPS)
    def _():
        for blk, dst in ((wg_blk, wg_ref), (wu_blk, wu_ref), (wd_blk, wd_ref), (win_blk, win_ref)):
            nr = blk.shape[0]
            dst[pl.ds(pl.multiple_of(step * nr, nr), nr), :] = blk[...].astype(BF16)

    @pl.when(step >= WEIGHT_STEPS)
    def _():
        _ffn_qkv_tile(x_ref, gpre_ref, gpost_ref, wg_ref, wu_ref, wd_ref, gmix_ref, win_ref,
                      cast_in, h_ref, qkv_ref, vt_ref, cast_out)


def _ffn_qkv_tile(x_ref, gpre_ref, gpost_ref, wg_ref, wu_ref, wd_ref, gmix_ref, win_ref,
                  cast_in, h_ref, qkv_ref, vt_ref, cast_out):
    for src, dst in zip(cast_in, cast_out):
        dst[...] = src[...].astype(BF16)
    rows = _sub_tiles()
    sub_blocks = ROW_TILE // ROW_SPLIT // LANES

    def prenorm(r, _):
        x = x_ref[rows[r], :]
        return x, _rms(x, gpre_ref[...]).astype(BF16)

    def ffn(r, st):
        return st[0], _swiglu(st[1], wg_ref, wu_ref, wd_ref)

    def residual(r, st):
        h = st[0] + 0.5 * _rms(st[1], gpost_ref[...])
        h_ref[rows[r], :] = h
        return _rms(h, gmix_ref[...]).astype(BF16)

    def project(r, u2):
        qkv = jnp.dot(u2, win_ref[...], preferred_element_type=F32)
        qkv_ref[rows[r], :] = qkv[:, :qkv_ref.shape[1]].astype(BF16)
        v_band = qkv[:, qkv_ref.shape[1]:]
        for j in range(sub_blocks):
            vt_ref[r * sub_blocks + j] = v_band[j * LANES:(j + 1) * LANES, :].T.astype(BF16)

    _emit_merged(_Chain(ROW_SPLIT, [prenorm, ffn, residual, project], skewed=True).steps)


def _sub_tiles():
    sub = ROW_TILE // ROW_SPLIT
    return [slice(r * sub, (r + 1) * sub) for r in range(ROW_SPLIT)]


def _resident(shape):
    nd = len(shape)
    return pl.BlockSpec(shape, lambda *_: (0,) * nd, pipeline_mode=pl.Buffered(1))


def _cast_block_spec(shape, n_steps, first_step):
    rows, cols = shape
    blk = next(r for r in range(BF16_SUBLANES, rows + 1, BF16_SUBLANES)
               if rows % r == 0 and r * n_steps >= rows)
    last = rows // blk - 1
    return pl.BlockSpec((blk, cols), lambda i: (jnp.clip(i - first_step, 0, last), 0))


def _ffn_qkv(x2, gpre, gpost, wg, wu, wd, gmix, win, later_weights):
    n, d = x2.shape
    n_steps = n // ROW_TILE
    d_qk = win.shape[1] - W_CH
    own = [wg, wu, wd, win]
    assert all(w.shape[0] % (WEIGHT_STEPS * BF16_SUBLANES) == 0 for w in own)
    tile_index = lambda i: jnp.maximum(i - WEIGHT_STEPS, 0)
    row = lambda w: pl.BlockSpec((ROW_TILE, w), lambda i: (tile_index(i), 0))
    vt_spec = pl.BlockSpec((ROW_TILE // LANES, W_CH, LANES), lambda i: (tile_index(i), 0, 0))
    own_specs = [pl.BlockSpec((w.shape[0] // WEIGHT_STEPS, w.shape[1]),
                              lambda i: (jnp.minimum(i, WEIGHT_STEPS - 1), 0)) for w in own]
    cast_specs = [_cast_block_spec(w.shape, n_steps, WEIGHT_STEPS) for w in later_weights]
    outs = pl.pallas_call(
        functools.partial(_ffn_qkv_kernel, len(later_weights)),
        grid=(WEIGHT_STEPS + n_steps,),
        in_specs=[row(d), _resident((1, d)), _resident((1, d)), *own_specs[:3],
                  _resident((1, d)), own_specs[3]] + cast_specs,
        out_specs=[row(d), row(d_qk), vt_spec] + cast_specs,
        out_shape=[jax.ShapeDtypeStruct((n, d), F32),
                   jax.ShapeDtypeStruct((n, d_qk), BF16),
                   jax.ShapeDtypeStruct((n // LANES, W_CH, LANES), BF16)]
                  + [jax.ShapeDtypeStruct(w.shape, BF16) for w in later_weights],
        scratch_shapes=[pltpu.VMEM(w.shape, BF16) for w in own],
        compiler_params=pltpu.CompilerParams(
            dimension_semantics=("arbitrary",), vmem_limit_bytes=DENSE_VMEM_BYTES),
        name="ffn1_qkv",
    )(x2, gpre, gpost, wg, wu, wd, gmix, win, *later_weights)
    return outs[0], outs[1], outs[2], outs[3:]


def _stack_heads(q2):
    lane = lax.broadcasted_iota(jnp.int32, q2.shape, 1)
    zero = jnp.zeros_like(q2)
    qs = q2 * jnp.asarray(SCALE, q2.dtype)
    return jnp.concatenate([jnp.where(lane < HEAD_DIM, qs, zero),
                            jnp.where(lane >= HEAD_DIM, qs, zero)], axis=0)


_NT = (((1,), (1,)), ((), ()))


class _Chain:
    def __init__(self, n_units, stages, skewed):
        self.state = [None] * n_units
        if skewed:
            order = [(s - k, k) for s in range(n_units + len(stages) - 1)
                     for k in range(len(stages)) if 0 <= s - k < n_units]
        else:
            order = [(u, k) for k in range(len(stages)) for u in range(n_units)]
        self.steps = [functools.partial(self._apply, stages[k], u) for u, k in order]

    def _apply(self, stage, u):
        self.state[u] = stage(u, self.state[u])


def _emit_merged(*step_lists):
    tagged = sorted(((i + 0.5) / len(steps), n, i)
                    for n, steps in enumerate(step_lists) for i in range(len(steps)))
    for _, n, i in tagged:
        step_lists[n][i]()


def _stickbreak_tile(q_ref, k_ref, v_ref, o_ref, carry_ref, qi, other_steps):
    t = SB_TILE
    npair = W_SB // LANES
    pair_cols = [slice(hp * LANES, (hp + 1) * LANES) for hp in range(npair)]

    neg_after = jnp.where(lax.broadcasted_iota(jnp.int32, (t, t), 0)
                          > lax.broadcasted_iota(jnp.int32, (t, t), 1), -1.0, 0.0).astype(BF16)
    q_stacks = [_stack_heads(q_ref[0, :, c]) for c in pair_cols]

    def sweep(blk, rows, causal=False, first=False, gate=None):
        r0, nr = rows
        start = pl.multiple_of(blk * t, t)
        heads_rows = (slice(r0, r0 + nr), slice(t + r0, t + r0 + nr))
        first_head = lax.broadcasted_iota(jnp.int32, (nr, LANES), 1) < HEAD_DIM
        if causal:
            ri = lax.broadcasted_iota(jnp.int32, (2 * nr, t), 0)
            ci = lax.broadcasted_iota(jnp.int32, (2 * nr, t), 1)
            before = ci < jnp.where(ri >= nr, ri - nr, ri) + r0

        def scores(hp, _):
            q = q_stacks[hp]
            if nr != t:
                q = jnp.concatenate([q[s] for s in heads_rows], axis=0)
            return lax.dot_general(q, k_ref[0, pl.ds(start, t), pair_cols[hp]], _NT,
                                   preferred_element_type=F32)

        def split(hp, z):
            sp = jnp.maximum(z, 0.0) + jnp.log(1.0 + jnp.exp(-jnp.abs(z)))
            if causal:
                sp = jnp.where(before, sp, 0.0)
            return z - sp, sp[:, 0:1], sp.astype(BF16)

        def cumsum(hp, st):
            return st[0], st[1], jnp.dot(st[2], neg_after, preferred_element_type=F32)

        def weights(hp, st):
            log_sig, sp_first, c = st
            p = jnp.exp(log_sig + c)
            if causal:
                p = jnp.where(before, p, 0.0)
            return c[:, 0:1] - sp_first, p.astype(BF16)

        def values(hp, st):
            return st[0], jnp.dot(st[1], v_ref[0, pl.ds(start, t), pair_cols[hp]],
                                  preferred_element_type=F32)

        def update(hp, st):
            tot, pv = st
            if first:
                new = tot
                o_ref[0, heads_rows[0], pair_cols[hp]] = jnp.where(first_head, pv[:nr], pv[nr:])
            else:
                old = jnp.concatenate([carry_ref[hp, s] for s in heads_rows], axis=0)
                scale = jnp.exp(old) if gate is None else jnp.exp(old) * gate
                pv, new = scale * pv, old + tot
                o_ref[0, heads_rows[0], pair_cols[hp]] += jnp.where(first_head, pv[:nr], pv[nr:])
            carry_ref[hp, heads_rows[0]] = new[:nr]
            carry_ref[hp, heads_rows[1]] = new[nr:]
            return new

        return _Chain(npair, [scores, split, cumsum, weights, values, update], skewed=True)

    def worst_carry(chain, rows, lo=None, hi=None):
        r0, nr = rows
        lo, hi = (r0 if lo is None else lo) - r0, (r0 + nr if hi is None else hi) - r0
        worst = None
        for c in chain.state:
            for part in (c[lo:hi], c[nr + lo:nr + hi]):
                worst = part if worst is None else jnp.maximum(worst, part)
        return jnp.max(worst)

    full, near, far = (0, t), (0, SB_NEAR_ROWS), (SB_NEAR_ROWS, t - SB_NEAR_ROWS)
    prev = jnp.maximum(qi - 1, 0)
    has_prev = qi > 0
    own = sweep(qi, full, causal=True, first=True)
    prev_near = sweep(prev, near, gate=has_prev.astype(F32))
    _emit_merged(own.steps + prev_near.steps, other_steps)

    worst_near = worst_carry(prev_near, near)
    worst_far = worst_carry(own, full, lo=SB_NEAR_ROWS)

    @pl.when(jnp.logical_and(has_prev, worst_far >= EXP_ZERO_BELOW))
    def _():
        _emit_merged(sweep(prev, far).steps)

    def cond(st):
        return jnp.logical_and(st[0] < qi, st[1] >= EXP_ZERO_BELOW)

    def body(st):
        chain = sweep(qi - 1 - st[0], full)
        _emit_merged(chain.steps)
        return st[0] + 1, worst_carry(chain, full)

    lax.while_loop(cond, body, (jnp.int32(1), jnp.maximum(worst_near, worst_far)))


def _bias_by_offset(rel_bias):
    h = rel_bias.shape[0]
    rb = rel_bias.astype(F32)
    far = LOOKBACK * CHUNK + CH_TQ - 1 - REL_CLIP
    near = CH_EXT - far - (2 * REL_CLIP + 1)
    ext = jnp.concatenate([jnp.broadcast_to(rb[:, -1:], (h, far)), rb[:, ::-1],
                           jnp.broadcast_to(rb[:, :1], (h, near))], axis=1)
    return ext.reshape(h, 1, CH_EXT)


def _build_bias_table(ext_ref, tab_ref):
    shape = (CH_TQ, CH_EXT)
    i = lax.broadcasted_iota(jnp.int32, shape, 0)
    j = lax.broadcasted_iota(jnp.int32, shape, 1)
    dchunk = j // CHUNK - i // CHUNK
    valid = (dchunk >= 0) & (dchunk <= LOOKBACK) & (j < CH_WIN)
    for h in range(H_CH):
        rolled = pltpu.roll(jnp.broadcast_to(ext_ref[h], shape), CH_EXT - (CH_TQ - 1), 1,
                            stride=1, stride_axis=0)
        tab = jnp.where(valid, rolled, NEG_INF)
        for o in range(CH_TAB_BLKS):
            tab_ref[h, o] = tab[:, o * LANES:(o + 1) * LANES].T


def _band_tiles(q_ref, k_ref, vt_ref, o_ref, tab_ref, qi):
    npair = W_CH // LANES
    units = [(sub, hp) for sub in range(CH_SUB) for hp in range(npair)]
    rows = [slice(sub * CH_TQ, (sub + 1) * CH_TQ) for sub in range(CH_SUB)]
    cols = [slice(hp * LANES, (hp + 1) * LANES) for hp in range(npair)]
    starts, shifts = [], []
    for sub in range(CH_SUB):
        nominal = (qi * CH_SUB + sub) * CH_TQ - LOOKBACK * CHUNK
        start = pl.multiple_of(jnp.maximum(nominal, 0), LANES)
        starts.append(start)
        shifts.append((start - nominal) // LANES)

    def scores(u, _):
        sub, hp = units[u]
        return lax.dot_general(k_ref[0, pl.ds(starts[sub], CH_WIN), cols[hp]],
                               _stack_heads(q_ref[0, rows[sub], cols[hp]]),
                               _NT, preferred_element_type=F32)

    def softmax(u, z):
        sub, hp = units[u]
        bias = jnp.concatenate(
            [jnp.concatenate([tab_ref[2 * hp + h, shifts[sub] + o] for h in range(2)], axis=-1)
             for o in range(CH_NBLK)], axis=0)
        z = z + bias
        m = jnp.max(z, axis=0, keepdims=True)
        p = jnp.exp(z - m)
        return jnp.sum(p, axis=0, keepdims=True), p.astype(BF16)

    def values(u, st):
        sub, hp = units[u]
        first_blk = starts[sub] // LANES
        vt = jnp.concatenate([vt_ref[0, first_blk + o, cols[hp], :] for o in range(CH_NBLK)],
                             axis=-1)
        return st[0], jnp.dot(vt, st[1], preferred_element_type=F32)

    def store(u, st):
        sub, hp = units[u]
        out = st[1] / st[0]
        own = jnp.concatenate([out[0:HEAD_DIM, 0:CH_TQ], out[HEAD_DIM:, CH_TQ:]], axis=0)
        o_ref[0, rows[sub], cols[hp]] = own.T

    return _Chain(len(units), [scores, softmax, values, store], skewed=False).steps


def _attn_kernel(ext_ref, qa_ref, ka_ref, va_ref, qb_ref, kb_ref, vb_ref, oa_ref, ob_ref,
                 carry_ref, tab_ref):
    @pl.when((pl.program_id(0) == 0) & (pl.program_id(1) == 0))
    def _():
        _build_bias_table(ext_ref, tab_ref)

    qi = pl.program_id(1)
    _stickbreak_tile(qa_ref, ka_ref, va_ref, oa_ref, carry_ref, qi,
                     _band_tiles(qb_ref, kb_ref, vb_ref, ob_ref, tab_ref, qi))


def _attention(qkv, vt_band, ext, b, s):
    assert W_SB == W_CH and SB_TILE == CH_SUB * CH_TQ
    t, w = SB_TILE, W_SB
    tile = lambda c: pl.BlockSpec((1, t, w), lambda bi, qi: (bi, qi, c))
    whole = lambda c: pl.BlockSpec((1, s, w), lambda bi, qi: (bi, 0, c))
    vt_spec = pl.BlockSpec((1,) + vt_band.shape[1:], lambda bi, qi: (bi, 0, 0, 0))
    return pl.pallas_call(
        _attn_kernel,
        grid=(b, s // t),
        in_specs=[_resident(ext.shape), tile(0), whole(1), whole(2), tile(3), whole(4), vt_spec],
        out_specs=[tile(0), tile(0)],
        out_shape=[jax.ShapeDtypeStruct((b, s, w), F32)] * 2,
        scratch_shapes=[pltpu.VMEM((w // LANES, 2 * t, 1), F32),
                        pltpu.VMEM((H_CH, CH_TAB_BLKS, LANES, CH_TQ), F32)],
        compiler_params=pltpu.CompilerParams(
            dimension_semantics=("arbitrary", "arbitrary"),
            vmem_limit_bytes=ATTN_VMEM_BYTES),
        name="mixers",
    )(ext, *([qkv] * 5), vt_band)


def _tail_kernel(h_ref, oa_ref, ob_ref, p_ref, gsb_ref, gch_ref, wout_ref, gmixpost_ref,
                 gpre_ref, gpost_ref, wg_ref, wu_ref, wd_ref, wple_ref, wgate_ref, gple_ref,
                 out_ref):
    rows = _sub_tiles()

    def mix_norm(r, _):
        return jnp.concatenate([_rms(oa_ref[rows[r], :], gsb_ref[...]).astype(BF16),
                                _rms(ob_ref[rows[r], :], gch_ref[...]).astype(BF16)], axis=-1)

    def out_proj(r, mixed):
        y = jnp.dot(mixed, wout_ref[...], preferred_element_type=F32)
        proj = jnp.dot(p_ref[rows[r], :].astype(BF16), wple_ref[...], preferred_element_type=F32)
        return y, proj

    def mix_residual(r, st):
        h = h_ref[rows[r], :] + _rms(st[0], gmixpost_ref[...])
        return h, st[1], _rms(h, gpre_ref[...]).astype(BF16)

    def ffn(r, st):
        return st[0], st[1], _swiglu(st[2], wg_ref, wu_ref, wd_ref)

    def ffn_residual(r, st):
        h = st[0] + 0.5 * _rms(st[2], gpost_ref[...])
        return h, st[1], h.astype(BF16)

    def ple_gate(r, st):
        return st[0], st[1], jnp.dot(st[2], wgate_ref[...], preferred_element_type=F32)

    def ple_residual(r, st):
        h, proj, gate = st
        e = proj * jax.nn.sigmoid(gate)
        out_ref[rows[r], :] = h + _rms(e, gple_ref[...])

    _emit_merged(_Chain(ROW_SPLIT, [mix_norm, out_proj, mix_residual, ffn, ffn_residual,
                                    ple_gate, ple_residual], skewed=True).steps)


def _tail(h, oa, ob, p2, gsb, gch, wout, gmixpost, gpre, gpost, wg, wu, wd, wple, wgate, gple):
    n, d = h.shape
    d_ff = wg.shape[1]
    row = lambda w: pl.BlockSpec((ROW_TILE, w), lambda i: (i, 0))
    return pl.pallas_call(
        _tail_kernel,
        grid=(n // ROW_TILE,),
        in_specs=[row(d), row(oa.shape[1]), row(ob.shape[1]), row(p2.shape[1]),
                  _resident((1, oa.shape[1])), _resident((1, ob.shape[1])),
                  _resident(wout.shape), _resident((1, d)),
                  _resident((1, d)), _resident((1, d)),
                  _resident((d, d_ff)), _resident((d, d_ff)), _resident((d_ff, d)),
                  _resident(wple.shape), _resident(wgate.shape), _resident((1, d))],
        out_specs=row(d),
        out_shape=jax.ShapeDtypeStruct((n, d), F32),
        compiler_params=pltpu.CompilerParams(
            dimension_semantics=("arbitrary",), vmem_limit_bytes=DENSE_VMEM_BYTES),
        name="outproj_ffn2_ple",
    )(h, oa, ob, p2, gsb, gch, wout, gmixpost, gpre, gpost, wg, wu, wd, wple, wgate, gple)


def kernel(x, p, g_ffn1_pre, g_ffn1_post, w_ffn1_gate, w_ffn1_up, w_ffn1_down, g_mix_pre, g_mix_post, w_in, g_out_sb, g_out_ch, rel_bias, w_out, g_ffn2_pre, g_ffn2_post, w_ffn2_gate, w_ffn2_up, w_ffn2_down, w_ple_proj, w_ple_gate, g_ple_post):
    b, s, d = x.shape
    depth = p.shape[0]
    n = b * s
    gain = lambda g: g.astype(F32).reshape(1, -1)
    h = x.reshape(n, d)
    for i in range(depth):
        later = [w_out[i], w_ffn2_gate[i], w_ffn2_up[i], w_ffn2_down[i],
                 w_ple_proj[i], w_ple_gate[i]]
        h1, qkv, vt_band, later_bf16 = _ffn_qkv(
            h, gain(g_ffn1_pre[i]), gain(g_ffn1_post[i]),
            w_ffn1_gate[i], w_ffn1_up[i], w_ffn1_down[i], gain(g_mix_pre[i]), w_in[i], later)
        wout, wg2, wu2, wd2, wple, wgate = later_bf16
        qkv3 = qkv.reshape(b, s, qkv.shape[1])
        vt_band = vt_band.reshape(b, s // LANES, W_CH, LANES)
        o_a, o_b = _attention(qkv3, vt_band, _bias_by_offset(rel_bias[i]), b, s)
        h = _tail(h1, o_a.reshape(n, W_SB), o_b.reshape(n, W_CH), p[i].reshape(n, -1),
                  gain(g_out_sb[i]), gain(g_out_ch[i]), wout, gain(g_mix_post[i]),
                  gain(g_ffn2_pre[i]), gain(g_ffn2_post[i]), wg2, wu2, wd2,
                  wple, wgate, gain(g_ple_post[i]))
    return h.reshape(b, s, d)
```
